```python
import jax
import jax.numpy as jnp
from jax import lax
import numpy as np

D_MODEL = 2048
BATCH = 1
SEQ = 16384
DEPTH = 2

RWKV_HEADS = 12
RWKV_HEAD = 64
RWKV_W = RWKV_HEADS * RWKV_HEAD
DECAY_LORA = 64
ICLR_LORA = 64
VRES_LORA = 32
GATE_LORA = 128
GN_EPS = 64e-5
LRU_BLOCKS = 6
LRU_BLOCK = 128
LRU_W = LRU_BLOCKS * LRU_BLOCK
CONV_W = 4
LRU_C = 8.0
MLA_HEADS = 4
QK_NOPE = 128
QK_ROPE = 64
V_HEAD = 128
Q_LORA = 384
KV_LORA = 128
MLA_W = MLA_HEADS * V_HEAD
ROPE_THETA = 10000.0
Q_BLOCK = 128
N_BRANCH = 3
N_EXPERTS = 16
EXPERT_FF = 1024
CAPACITY_FACTOR = 2
NORM_EPS = 1e-6

RWKV_SIZES = (RWKV_W, RWKV_W, RWKV_W, DECAY_LORA, DECAY_LORA, ICLR_LORA, ICLR_LORA, GATE_LORA)
RWKV_COLS = 3 * RWKV_W + 2 * DECAY_LORA + 2 * ICLR_LORA + GATE_LORA
REST_SIZES = (LRU_W, LRU_W, Q_LORA, KV_LORA, QK_ROPE, N_BRANCH * D_MODEL)
IN_COLS = RWKV_COLS + 2 * LRU_W + Q_LORA + KV_LORA + QK_ROPE + N_BRANCH * D_MODEL

kernel_name = "bidir_hybrid_rwkv7_rglru_mla_ecmoe"


def rmsnorm(x, g):
    xf = x.astype(jnp.float32)
    y = xf * lax.rsqrt(jnp.mean(xf * xf, axis=-1, keepdims=True) + NORM_EPS)
    return (y * g.astype(jnp.float32)).astype(x.dtype)


def split_cols(t, sizes):
    return jnp.split(t, np.cumsum(sizes)[:-1].tolist(), axis=-1)


def centred_token_shift(p, mu):
    prev = jnp.pad(p, ((0, 0), (1, 0), (0, 0)))[:, :-1]
    nxt = jnp.pad(p, ((0, 0), (0, 1), (0, 0)))[:, 1:]
    return p + mu[0] * (prev - p) + mu[1] * (nxt - p)


def rwkv7_scan(r, w, k, v, kk, a, reverse):
    B, _, H, N = r.shape

    def step(state, inp):
        r_t, w_t, k_t, v_t, kk_t, a_t = inp
        sa = jnp.einsum('bhij,bhj->bhi', state, -kk_t)
        state = (state * w_t[:, :, None, :]
                 + sa[..., None] * (kk_t * a_t)[:, :, None, :]
                 + v_t[..., None] * k_t[:, :, None, :])
        return state, jnp.einsum('bhij,bhj->bhi', state, r_t)

    xs = tuple(jnp.swapaxes(t, 0, 1) for t in (r, w, k, v, kk, a))
    state0 = jnp.zeros((B, H, N, N), jnp.float32)
    _, ys = lax.scan(step, state0, xs, reverse=reverse)
    return jnp.swapaxes(ys, 0, 1)


def rwkv7_mixer(r, k, v, wd, ad, gd, w0, w2, a0, a2, g2, k_k, k_a, r_k, lnx_w, lnx_b):
    B, S, _ = r.shape
    f32 = jnp.float32

    def heads(t):
        return t.astype(f32).reshape(B, S, RWKV_HEADS, RWKV_HEAD)

    r_h, v_h = heads(r), heads(v)
    kk = heads(k * k_k)
    kk = kk / jnp.maximum(jnp.sqrt(jnp.sum(kk * kk, axis=-1, keepdims=True)), 1e-12)
    ys, ks = [], []
    for d in range(2):
        w_log = -jax.nn.softplus(-(w0[d] + jnp.tanh(wd[d]) @ w2[d]).astype(f32)) - 0.5
        decay = heads(jnp.exp(-jnp.exp(w_log)))
        a = jax.nn.sigmoid((a0[d] + ad[d] @ a2[d]).astype(f32))
        k_d = heads(k.astype(f32) * (1.0 + (a - 1.0) * k_a.astype(f32)))
        ys.append(rwkv7_scan(r_h, decay, k_d, v_h, kk, heads(a), reverse=(d == 1)))
        ks.append(k_d)
    y = ys[0] + ys[1]
    mu = jnp.mean(y, axis=-1, keepdims=True)
    var = jnp.mean(jnp.square(y - mu), axis=-1, keepdims=True)
    y = ((y - mu) * lax.rsqrt(var + GN_EPS)).reshape(B, S, RWKV_W) * lnx_w.astype(f32) + lnx_b.astype(f32)
    bonus = jnp.sum(r_h * (ks[0] + ks[1]) * r_k.astype(f32), axis=-1, keepdims=True) * v_h
    g = (jax.nn.sigmoid(gd) @ g2).astype(f32)
    return ((y + bonus.reshape(B, S, RWKV_W)) * g).astype(r.dtype)


def linear_scan(a, b, reverse):
    def combine(left, right):
        a_l, b_l = left
        a_r, b_r = right
        return a_l * a_r, a_r * b_l + b_r

    _, h = lax.associative_scan(combine, (a, b), reverse=reverse, axis=1)
    return h


def rglru_mixer(xb, gb, conv_w, conv_b, wa, ba, wx, bx, lam):
    B, S, C = xb.shape
    f32 = jnp.float32
    xc = lax.conv_general_dilated(
        xb, conv_w[:, None, :], window_strides=(1,),
        padding=[((CONV_W - 1) // 2, CONV_W // 2)],
        dimension_numbers=('NWC', 'WIO', 'NWC'), feature_group_count=C) + conv_b
    xblk = xc.reshape(B, S, LRU_BLOCKS, LRU_BLOCK)
    xc32 = xc.astype(f32)
    hs = []
    for d in range(2):
        rg = jax.nn.sigmoid((jnp.einsum('bsgi,gij->bsgj', xblk, wa[d]).reshape(B, S, C) + ba[d]).astype(f32))
        ig = jax.nn.sigmoid((jnp.einsum('bsgi,gij->bsgj', xblk, wx[d]).reshape(B, S, C) + bx[d]).astype(f32))
        log_a = -LRU_C * rg * jax.nn.softplus(-lam[d].astype(f32))
        a_t = jnp.exp(log_a)
        b_t = jnp.sqrt(-jnp.expm1(2.0 * log_a)) * (ig * xc32)
        hs.append(linear_scan(a_t, b_t, reverse=(d == 1)))
    return ((hs[0] + hs[1]) * jax.nn.gelu(gb.astype(f32))).astype(xb.dtype)


def rope_tables(positions):
    inv = ROPE_THETA ** (-jnp.arange(0, QK_ROPE, 2, dtype=jnp.float32) / QK_ROPE)
    ang = positions.astype(jnp.float32)[..., None] * inv
    return jnp.cos(ang), jnp.sin(ang)


def apply_rope(t, cos, sin):
    half = t.shape[-1] // 2
    tf = t.astype(jnp.float32)
    t1, t2 = tf[..., :half], tf[..., half:]
    c, s = cos[:, :, None, :], sin[:, :, None, :]
    return jnp.concatenate([t1 * c - t2 * s, t1 * s + t2 * c], axis=-1).astype(t.dtype)


def bidirectional_attention(q, k, v, scale):
    B, S, H, Dq = q.shape
    nb = S // Q_BLOCK
    q_blocks = jnp.moveaxis(q.reshape(B, nb, Q_BLOCK, H, Dq), 1, 0)

    def attend(qb):
        s = jnp.einsum('bqhd,bkhd->bhqk', qb, k, preferred_element_type=jnp.float32) * scale
        p = jax.nn.softmax(s, axis=-1).astype(v.dtype)
        return jnp.einsum('bhqk,bkhd->bqhd', p, v)

    o = lax.map(attend, q_blocks)
    return jnp.moveaxis(o, 0, 1).reshape(B, S, H, v.shape[-1])


def mla_mixer(cq, ckv, kr, q_norm, kv_norm, w_uq, w_uk, w_uv, cos, sin):
    B, S, _ = cq.shape
    q = (rmsnorm(cq, q_norm) @ w_uq).reshape(B, S, MLA_HEADS, QK_NOPE + QK_ROPE)
    c_kv = rmsnorm(ckv, kv_norm)
    k_nope = (c_kv @ w_uk).reshape(B, S, MLA_HEADS, QK_NOPE)
    v = (c_kv @ w_uv).reshape(B, S, MLA_HEADS, V_HEAD)
    q = jnp.concatenate([q[..., :QK_NOPE], apply_rope(q[..., QK_NOPE:], cos, sin)], axis=-1)
    k_pe = apply_rope(kr[:, :, None, :], cos, sin)
    k = jnp.concatenate([k_nope, jnp.broadcast_to(k_pe, (B, S, MLA_HEADS, QK_ROPE))], axis=-1)
    o = bidirectional_attention(q, k, v, (QK_NOPE + QK_ROPE) ** -0.5)
    return o.reshape(B, S, MLA_W)


def expert_choice_moe(x, w_router, w_gate, w_up, w_down):
    B, S, D = x.shape
    cap = CAPACITY_FACTOR * S // N_EXPERTS
    aff = jax.nn.softmax((x @ w_router).astype(jnp.float32), axis=-1)
    gate, idx = lax.top_k(jnp.swapaxes(aff, 1, 2), cap)
    xs = jax.vmap(lambda xb, ib: xb[ib])(x, idx)
    hdn = jax.nn.silu(jnp.einsum('becd,edf->becf', xs, w_gate)) * jnp.einsum('becd,edf->becf', xs, w_up)
    ye = jnp.einsum('becf,efd->becd', hdn, w_down) * gate[..., None].astype(x.dtype)
    flat = (jnp.arange(B, dtype=jnp.int32)[:, None, None] * S + idx).reshape(-1)
    out = jnp.zeros((B * S, D), x.dtype).at[flat].add(ye.reshape(-1, D))
    return out.reshape(B, S, D)


def setup_inputs(seed: int = 0) -> dict:
    key = jax.random.key(seed)
    keys = iter(jax.random.split(key, 64))
    f32 = jnp.float32

    def normal(shape, scale):
        return jax.random.normal(next(keys), shape, f32) * scale

    def uniform(shape, lo, hi):
        return jax.random.uniform(next(keys), shape, f32, lo, hi)

    def gain(shape):
        return 1.0 + normal(shape, 0.02)

    L, D = DEPTH, D_MODEL
    u = uniform((L, 2, LRU_W), 0.9, 0.999)
    a_base = u ** (1.0 / LRU_C)
    return {
        'x': normal((BATCH, SEQ, D), 1.0),
        'positions': jnp.broadcast_to(jnp.arange(SEQ, dtype=jnp.int32), (BATCH, SEQ)),
        'norm_mix': gain((L, D)),
        'w_in': normal((L, D, IN_COLS), D ** -0.5),
        'rwkv_shift_mu': uniform((L, 2, RWKV_COLS), 0.0, 0.5),
        'rwkv_w0': uniform((L, 2, RWKV_W), -5.0, -0.5),
        'rwkv_w2': normal((L, 2, DECAY_LORA, RWKV_W), 0.1 * DECAY_LORA ** -0.5),
        'rwkv_a0': normal((L, 2, RWKV_W), 0.1),
        'rwkv_a2': normal((L, 2, ICLR_LORA, RWKV_W), ICLR_LORA ** -0.5),
        'rwkv_g2': normal((L, GATE_LORA, RWKV_W), GATE_LORA ** -0.5),
        'rwkv_k_k': 0.85 + normal((L, RWKV_W), 0.05),
        'rwkv_k_a': 1.0 + normal((L, RWKV_W), 0.05),
        'rwkv_r_k': normal((L, RWKV_HEADS, RWKV_HEAD), 0.1),
        'rwkv_lnx_w': gain((L, RWKV_W)),
        'rwkv_lnx_b': normal((L, RWKV_W), 0.01),
        'rwkv_vres_down': normal((L - 1, D, VRES_LORA), D ** -0.5),
        'rwkv_vres_up': normal((L - 1, VRES_LORA, RWKV_W), VRES_LORA ** -0.5),
        'rwkv_vres_b': 1.0 + normal((L - 1, RWKV_W), 0.1),
        'lru_conv_w': normal((L, CONV_W, LRU_W), 0.5),
        'lru_conv_b': normal((L, LRU_W), 0.01),
        'lru_wa': normal((L, 2, LRU_BLOCKS, LRU_BLOCK, LRU_BLOCK), LRU_BLOCK ** -0.5),
        'lru_ba': normal((L, 2, LRU_W), 0.01),
        'lru_wx': normal((L, 2, LRU_BLOCKS, LRU_BLOCK, LRU_BLOCK), LRU_BLOCK ** -0.5),
        'lru_bx': normal((L, 2, LRU_W), 0.01),
        'lru_lambda': jnp.log(a_base) - jnp.log1p(-a_base),
        'mla_q_norm': gain((L, Q_LORA)),
        'mla_kv_norm': gain((L, KV_LORA)),
        'mla_w_uq': normal((L, Q_LORA, MLA_HEADS * (QK_NOPE + QK_ROPE)), Q_LORA ** -0.5),
        'mla_w_uk': normal((L, KV_LORA, MLA_HEADS * QK_NOPE), KV_LORA ** -0.5),
        'mla_w_uv': normal((L, KV_LORA, MLA_HEADS * V_HEAD), KV_LORA ** -0.5),
        'wo_rwkv': normal((L, RWKV_W, D), RWKV_W ** -0.5),
        'wo_lru': normal((L, LRU_W, D), LRU_W ** -0.5),
        'wo_mla': normal((L, MLA_W, D), MLA_W ** -0.5),
        'w_out': normal((L, D, D), D ** -0.5),
        'norm_ffn': gain((L, D)),
        'w_router': normal((L, D, N_EXPERTS), D ** -0.5),
        'w_gate': normal((L, N_EXPERTS, D, EXPERT_FF), D ** -0.5),
        'w_up': normal((L, N_EXPERTS, D, EXPERT_FF), D ** -0.5),
        'w_down': normal((L, N_EXPERTS, EXPERT_FF, D), EXPERT_FF ** -0.5),
        'norm_final': gain((D,)),
    }


def reference(x, positions, norm_mix, w_in, rwkv_shift_mu, rwkv_w0, rwkv_w2, rwkv_a0, rwkv_a2,
              rwkv_g2, rwkv_k_k, rwkv_k_a, rwkv_r_k, rwkv_lnx_w, rwkv_lnx_b, rwkv_vres_down,
              rwkv_vres_up, rwkv_vres_b, lru_conv_w, lru_conv_b, lru_wa, lru_ba, lru_wx, lru_bx,
              lru_lambda, mla_q_norm, mla_kv_norm, mla_w_uq, mla_w_uk, mla_w_uv, wo_rwkv, wo_lru,
              wo_mla, w_out, norm_ffn, w_router, w_gate, w_up, w_down, norm_final):
    B, S, D = x.shape
    cos, sin = rope_tables(positions)
    v_first = None
    for l in range(DEPTH):
        xn = rmsnorm(x, norm_mix[l])
        h = xn @ w_in[l]
        h_rwkv = centred_token_shift(h[..., :RWKV_COLS], rwkv_shift_mu[l])
        r, k, v, wd_f, wd_b, ad_f, ad_b, gd = split_cols(h_rwkv, RWKV_SIZES)
        xb, gb, cq, ckv, kr, gates = split_cols(h[..., RWKV_COLS:], REST_SIZES)
        if l == 0:
            v_first = v
        else:
            vg = jax.nn.sigmoid(rwkv_vres_b[l - 1] + (xn @ rwkv_vres_down[l - 1]) @ rwkv_vres_up[l - 1])
            v = v + (v_first - v) * vg
        y_a = rwkv7_mixer(r, k, v, (wd_f, wd_b), (ad_f, ad_b), gd, rwkv_w0[l], rwkv_w2[l],
                          rwkv_a0[l], rwkv_a2[l], rwkv_g2[l], rwkv_k_k[l], rwkv_k_a[l],
                          rwkv_r_k[l], rwkv_lnx_w[l], rwkv_lnx_b[l])
        y_b = rglru_mixer(xb, gb, lru_conv_w[l], lru_conv_b[l], lru_wa[l], lru_ba[l],
                          lru_wx[l], lru_bx[l], lru_lambda[l])
        y_c = mla_mixer(cq, ckv, kr, mla_q_norm[l], mla_kv_norm[l], mla_w_uq[l],
                        mla_w_uk[l], mla_w_uv[l], cos, sin)
        g = jax.nn.sigmoid(gates).reshape(B, S, N_BRANCH, D)
        merged = (g[:, :, 0] * (y_a @ wo_rwkv[l])
                  + g[:, :, 1] * (y_b @ wo_lru[l])
                  + g[:, :, 2] * (y_c @ wo_mla[l]))
        x = x + merged @ w_out[l]
        x = x + expert_choice_moe(rmsnorm(x, norm_ffn[l]), w_router[l], w_gate[l], w_up[l], w_down[l])
    return rmsnorm(x, norm_final)
```

```python
import functools

import numpy as np
import jax
import jax.numpy as jnp
from jax import lax
from jax.experimental import pallas as pl
from jax.experimental.pallas import tpu as pltpu

F32 = jnp.float32
BF16 = jnp.bfloat16

D_MODEL = 2048
RWKV_HEADS = 12
RWKV_HEAD = 64
RWKV_W = RWKV_HEADS * RWKV_HEAD
DECAY_LORA = 64
ICLR_LORA = 64
VRES_LORA = 32
GATE_LORA = 128
GN_EPS = 64e-5
LRU_BLOCKS = 6
LRU_BLOCK = 128
LRU_W = LRU_BLOCKS * LRU_BLOCK
CONV_W = 4
LRU_C = 8.0
MLA_HEADS = 4
QK_NOPE = 128
QK_ROPE = 64
V_HEAD = 128
Q_LORA = 384
KV_LORA = 128
MLA_W = MLA_HEADS * V_HEAD
ROPE_THETA = 10000.0
N_BRANCH = 3
N_EXPERTS = 16
EXPERT_FF = 1024
CAPACITY_FACTOR = 2
NORM_EPS = 1e-6
RWKV_COLS = 3 * RWKV_W + 2 * DECAY_LORA + 2 * ICLR_LORA + GATE_LORA

LANES = 128
SUBLANES = 8
VMEM_LIMIT = 56 * 1024 * 1024

SEG_LRU = RWKV_COLS
SEG_MLA = SEG_LRU + 2 * LRU_W
SEG_GATES = SEG_MLA + Q_LORA + KV_LORA + QK_ROPE
MLA_SEG_W = Q_LORA + KV_LORA + 2 * LANES
MLA_COL_KR = Q_LORA + KV_LORA
MLA_COL_VRES = MLA_COL_KR + LANES

CHUNK = 64
PAIR = 2 * RWKV_HEAD


def _cparams(sem):
    return pltpu.CompilerParams(dimension_semantics=sem, vmem_limit_bytes=VMEM_LIMIT)


def _dot(a, b):
    return jnp.dot(a, b, preferred_element_type=F32)


def _dot_nt(a, b):
    return lax.dot_general(a, b, (((1,), (1,)), ((), ())), preferred_element_type=F32)


def _dot_tn(a, b):
    return lax.dot_general(a, b, (((0,), (0,)), ((), ())), preferred_element_type=F32)


def _split(x):
    hi = x.astype(BF16)
    lo = (x - hi.astype(F32)).astype(BF16)
    return hi, lo


def _dot_exactb(x, b):
    hi, lo = _split(x)
    return _dot(hi, b) + _dot(lo, b)


def _dot3(a, b):
    ah, al = _split(a)
    bh, bl = _split(b)
    return _dot(ah, bh) + (_dot(ah, bl) + _dot(al, bh))


def _dot3_nt(a, b):
    ah, al = _split(a)
    bh, bl = _split(b)
    return _dot_nt(ah, bh) + (_dot_nt(ah, bl) + _dot_nt(al, bh))


def _dot3_tn(a, b):
    ah, al = _split(a)
    bh, bl = _split(b)
    return _dot_tn(ah, bh) + (_dot_tn(ah, bl) + _dot_tn(al, bh))


def _sigmoid(x):
    return 1.0 / (1.0 + jnp.exp(-x))


def _softplus(x):
    return jnp.maximum(x, 0.0) + jnp.log(1.0 + jnp.exp(-jnp.abs(x)))


def _rmsnorm_kernel(x_ref, g_ref, o_ref):
    x = x_ref[...]
    ms = jnp.mean(x * x, axis=-1, keepdims=True)
    o_ref[...] = (x * lax.rsqrt(ms + NORM_EPS) * g_ref[...]).astype(o_ref.dtype)


def _rmsnorm(x, gain, dtype):
    S, D = x.shape
    tm = min(512, S)
    return pl.pallas_call(
        _rmsnorm_kernel,
        grid=(S // tm,),
        in_specs=[pl.BlockSpec((tm, D), lambda i: (i, 0)), pl.BlockSpec((1, D), lambda i: (0, 0))],
        out_specs=pl.BlockSpec((tm, D), lambda i: (i, 0)),
        out_shape=jax.ShapeDtypeStruct((S, D), dtype),
        compiler_params=_cparams(("parallel",)),
        name="rmsnorm",
    )(x, gain.reshape(1, D))


def _matmul_kernel(has_res, *refs):
    if has_res:
        a_ref, w_ref, res_ref, o_ref = refs
        o_ref[...] = res_ref[...] + _dot(a_ref[...], w_ref[...])
    else:
        a_ref, w_ref, o_ref = refs
        o_ref[...] = _dot(a_ref[...], w_ref[...])


def _matmul(a, w, tn, res=None, name="matmul"):
    M, K = a.shape
    N = w.shape[1]
    tm = min(512, M)
    in_specs = [pl.BlockSpec((tm, K), lambda j, i: (i, 0)), pl.BlockSpec((K, tn), lambda j, i: (0, j))]
    args = [a, w]
    if res is not None:
        in_specs.append(pl.BlockSpec((tm, tn), lambda j, i: (i, j)))
        args.append(res)
    return pl.pallas_call(
        functools.partial(_matmul_kernel, res is not None),
        grid=(N // tn, M // tm),
        in_specs=in_specs,
        out_specs=pl.BlockSpec((tm, tn), lambda j, i: (i, j)),
        out_shape=jax.ShapeDtypeStruct((M, N), F32),
        compiler_params=_cparams(("parallel", "parallel")),
        name=name,
    )(*args)


def _in_projection(xn, w_in, vres_down):
    w = w_in.astype(BF16)
    w_mla = jnp.zeros((D_MODEL, MLA_SEG_W), BF16)
    w_mla = w_mla.at[:, :MLA_COL_KR + QK_ROPE].set(w[:, SEG_MLA:SEG_GATES])
    if vres_down is not None:
        w_mla = w_mla.at[:, MLA_COL_VRES:MLA_COL_VRES + VRES_LORA].set(vres_down.astype(BF16))
    h_rwkv = _matmul(xn, w[:, :SEG_LRU], RWKV_COLS, name="inproj_rwkv")
    h_lru = _matmul(xn, w[:, SEG_LRU:SEG_MLA], 2 * LRU_W, name="inproj_lru")
    h_mla = _matmul(xn, w_mla, MLA_SEG_W, name="inproj_mla")
    gates = _matmul(xn, w[:, SEG_GATES:], D_MODEL, name="inproj_gates")
    return h_rwkv, h_lru, h_mla, gates


def _rwkv_prep_kernel(has_vres, tm, nblk, *refs):
    if has_vres:
        (h_ref, hp_ref, hn_ref, lora_ref, vfirst_ref, vup_ref, vb_ref, *rest) = refs
    else:
        (h_ref, hp_ref, hn_ref, *rest) = refs
    (mu_ref, w0_ref, w2_ref, a0_ref, a2_ref, g2_ref, kkw_ref, ka_ref, rk_ref, bd_ref,
     r_o, v_o, kk_o, lw_o, kd_o, a_o, g_o, bonus_o) = rest
    i = pl.program_id(0)
    p = h_ref[...]
    rows = lax.broadcasted_iota(jnp.int32, (tm, 1), 0)
    prev_row = jnp.where(i == 0, 0.0, hp_ref[SUBLANES - 1:SUBLANES, :])
    next_row = jnp.where(i == nblk - 1, 0.0, hn_ref[0:1, :])
    prev = jnp.where(rows == 0, prev_row, pltpu.roll(p, 1, 0))
    nxt = jnp.where(rows == tm - 1, next_row, pltpu.roll(p, tm - 1, 0))
    hs = p + mu_ref[0:1, :] * (prev - p) + mu_ref[1:2, :] * (nxt - p)

    W = RWKV_W
    r = hs[:, 0:W]
    k = hs[:, W:2 * W]
    v = hs[:, 2 * W:3 * W]
    wd = hs[:, 3 * W:3 * W + LANES]
    ad = hs[:, 3 * W + LANES:3 * W + 2 * LANES]
    gd = hs[:, 3 * W + 2 * LANES:3 * W + 3 * LANES]
    if has_vres:
        vg = _sigmoid(vb_ref[...] + _dot(lora_ref[...].astype(BF16), vup_ref[...]))
        v = v + (vfirst_ref[...] - v) * vg
    bd = bd_ref[...]
    kk = k * kkw_ref[...]
    n2 = _dot_exactb(kk * kk, bd)
    kk = kk / jnp.maximum(jnp.sqrt(n2), 1e-12)
    tw = jnp.tanh(wd).astype(BF16)
    adb = ad.astype(BF16)
    ksum = None
    for d in range(2):
        wl = w0_ref[d:d + 1, :] + _dot(tw, w2_ref[d])
        w_log = -_softplus(-wl) - 0.5
        lw_o[d] = -jnp.exp(w_log)
        a = _sigmoid(a0_ref[d:d + 1, :] + _dot(adb, a2_ref[d]))
        kd = k * (1.0 + (a - 1.0) * ka_ref[...])
        a_o[d] = a
        kd_o[d] = kd
        ksum = kd if ksum is None else ksum + kd
    r_o[...] = r
    v_o[...] = v
    kk_o[...] = kk
    g_o[...] = _dot(_sigmoid(gd).astype(BF16), g2_ref[...])
    bonus_o[...] = _dot_exactb(r * ksum * rk_ref[...], bd) * v


def _head_blockdiag():
    hid = np.arange(RWKV_W) // RWKV_HEAD
    return jnp.asarray((hid[:, None] == hid[None, :]).astype(np.float32), dtype=BF16)


def _rwkv_prep(h, h_mla, l, p, v_first):
    S = h.shape[0]
    tm = min(256, S)
    nblk = S // tm
    has_vres = l > 0
    W = RWKV_W
    hb = tm // SUBLANES
    nrow8 = S // SUBLANES

    def zpad(w, rows_before):
        z = jnp.zeros((LANES, W), F32)
        return z.at[rows_before:rows_before + w.shape[0]].set(w)

    w2 = jnp.stack([zpad(p['rwkv_w2'][l, 0], 0), zpad(p['rwkv_w2'][l, 1], DECAY_LORA)]).astype(BF16)
    a2 = jnp.stack([zpad(p['rwkv_a2'][l, 0], 0), zpad(p['rwkv_a2'][l, 1], ICLR_LORA)]).astype(BF16)
    const2 = lambda shape: pl.BlockSpec(shape, lambda i: (0,) * len(shape))
    in_specs = [
        pl.BlockSpec((tm, RWKV_COLS), lambda i: (i, 0)),
        pl.BlockSpec((SUBLANES, RWKV_COLS), lambda i: (jnp.maximum(i * hb - 1, 0), 0)),
        pl.BlockSpec((SUBLANES, RWKV_COLS), lambda i: (jnp.minimum((i + 1) * hb, nrow8 - 1), 0)),
    ]
    args = [h, h, h]
    if has_vres:
        in_specs += [
            pl.BlockSpec((tm, LANES), lambda i: (i, MLA_COL_VRES // LANES)),
            pl.BlockSpec((tm, W), lambda i: (i, 0)),
            const2((LANES, W)),
            const2((1, W)),
        ]
        vup = jnp.zeros((LANES, W), F32).at[:VRES_LORA].set(p['rwkv_vres_up'][l - 1]).astype(BF16)
        args += [h_mla, v_first, vup, p['rwkv_vres_b'][l - 1].reshape(1, W)]
    in_specs += [
        const2((2, RWKV_COLS)), const2((2, W)), const2((2, LANES, W)), const2((2, W)), const2((2, LANES, W)),
        const2((LANES, W)), const2((1, W)), const2((1, W)), const2((1, W)), const2((W, W)),
    ]
    args += [
        p['rwkv_shift_mu'][l], p['rwkv_w0'][l], w2, p['rwkv_a0'][l], a2,
        p['rwkv_g2'][l].astype(BF16), p['rwkv_k_k'][l].reshape(1, W), p['rwkv_k_a'][l].reshape(1, W),
        p['rwkv_r_k'][l].reshape(1, W), _head_blockdiag(),
    ]
    row = pl.BlockSpec((tm, W), lambda i: (i, 0))
    row2 = pl.BlockSpec((2, tm, W), lambda i: (0, i, 0))
    sds = jax.ShapeDtypeStruct((S, W), F32)
    sds2 = jax.ShapeDtypeStruct((2, S, W), F32)
    return pl.pallas_call(
        functools.partial(_rwkv_prep_kernel, has_vres, tm, nblk),
        grid=(nblk,),
        in_specs=in_specs,
        out_specs=[row, row, row, row2, row2, row2, row, row],
        out_shape=[sds, sds, sds, sds2, sds2, sds2, sds, sds],
        compiler_params=_cparams(("parallel",)),
        name="rwkv_prep",
    )(*args)


N_MASK = 8


def _scan_masks():
    n = PAIR
    out = np.zeros((2, N_MASK, n, n), np.float32)
    r = np.arange(n)[:, None]
    c = np.arange(n)[None, :]
    same = (r // CHUNK) == (c // CHUNK)
    for d in range(2):
        before = (c < r) if d == 0 else (c > r)
        strict = same & before
        out[d, 0] = strict
        out[d, 1] = same & (before | (c == r))
        out[d, 2] = strict & ((r // 8) == (c // 8))
        for j, b in enumerate((8, 16, 32)):
            out[d, 3 + j] = strict & ((r // (2 * b)) == (c // (2 * b))) & ((r // b) != (c // b))
        out[d, 6] = (r == c)
        out[d, 7] = (before | (c == r)) & (r < CHUNK) & (c < CHUNK)
    return jnp.asarray(out)


def _rwkv_scan_kernel(nsub, r_ref, v_ref, kk_ref, lw_ref, k_ref, a_ref, m_ref, y_ref, s_ref):
    T = CHUNK
    d = pl.program_id(0)
    rev = d == 1

    @pl.when(pl.program_id(2) == 0)
    def _():
        s_ref[...] = jnp.zeros_like(s_ref)

    lane = lax.broadcasted_iota(jnp.int32, (1, PAIR), 1)
    m0 = jnp.where(lane < RWKV_HEAD, 1.0, 0.0)
    m1 = 1.0 - m0

    def stack(x):
        return jnp.concatenate([x * m0, x * m1], axis=0)

    def chunk_body(j, carry):
        jj = jnp.where(rev, nsub - 1 - j, j)
        off = pl.multiple_of(jj * T, T)
        sl = pl.ds(off, T)
        strict, incl, eye = m_ref[0], m_ref[1], m_ref[6]
        lw = lw_ref[sl, :]
        kk = kk_ref[sl, :]
        beta = a_ref[sl, :] * kk
        kd = k_ref[sl, :]
        cs = m_ref[7][0:T, 0:T].astype(BF16)
        lh, ll = _split(lw)
        c = _dot(cs, lh) + _dot(cs, ll)
        c_tot = jnp.where(rev, c[0:1, :], c[T - 1:T, :])
        e_inv = jnp.exp(-c)
        e_fin = jnp.exp(c_tot - c)
        kt = stack(kk * jnp.exp(c - lw))
        rt = stack(r_ref[sl, :] * jnp.exp(c))
        bb = stack(beta * e_inv)
        kb = stack(kd * e_inv)
        bh = stack(beta * e_fin)
        kh = stack(kd * e_fin)
        vs = stack(v_ref[sl, :])

        p1 = _dot3_nt(jnp.concatenate([kt, rt], axis=0), jnp.concatenate([bb, kb], axis=0))
        A = p1[0:PAIR, 0:PAIR] * strict
        B = p1[0:PAIR, PAIR:] * strict
        C = p1[PAIR:, 0:PAIR] * incl
        E = p1[PAIR:, PAIR:] * incl

        d8 = A * m_ref[2]
        d2 = _dot3(d8, d8)
        d4 = _dot3(d2, d2)
        x = _dot3(eye - d8, eye + d2)
        x = _dot3(x, eye + d4)
        for lvl in range(3):
            aoff = A * m_ref[3 + lvl]
            x = x - _dot3(x, _dot3(aoff, x))

        bv = _dot3(B, vs)
        mu = -_dot3(x, jnp.concatenate([kt, bv], axis=1))
        cm = _dot3(C, mu)
        rm = rt + cm[:, 0:PAIR]
        y0 = cm[:, PAIR:] + _dot3(E, vs)
        gh = _dot3_tn(bh, mu)
        G = gh[:, 0:PAIR] + eye * jnp.exp(c_tot)
        H = gh[:, PAIR:] + _dot3_tn(kh, vs)
        s0 = s_ref[...]
        yst = _dot3(rm, s0) + y0
        y_ref[sl, :] = yst[0:T, :] + yst[T:, :]
        s_ref[...] = _dot3(G, s0) + H
        return carry

    lax.fori_loop(0, nsub, chunk_body, 0)


def _rwkv_scan(r, v, kk, lw, kd, a):
    S = r.shape[0]
    tb = min(256, S)
    nsub = tb // CHUNK
    nblk = S // tb
    npair = RWKV_W // PAIR
    rowmap = lambda d, pr, c: (jnp.where(d == 1, nblk - 1 - c, c), pr)
    rowmap2 = lambda d, pr, c: (d, jnp.where(d == 1, nblk - 1 - c, c), pr)
    shared = pl.BlockSpec((tb, PAIR), rowmap)
    perdir = pl.BlockSpec((None, tb, PAIR), rowmap2)
    return pl.pallas_call(
        functools.partial(_rwkv_scan_kernel, nsub),
        grid=(2, npair, nblk),
        in_specs=[shared, shared, shared, perdir, perdir, perdir,
                  pl.BlockSpec((None, N_MASK, PAIR, PAIR), lambda d, pr, c: (d, 0, 0, 0))],
        out_specs=perdir,
        out_shape=jax.ShapeDtypeStruct((2, S, RWKV_W), F32),
        scratch_shapes=[pltpu.VMEM((PAIR, PAIR), F32)],
        compiler_params=_cparams(("parallel", "parallel", "arbitrary")),
        name="rwkv_scan",
    )(r, v, kk, lw, kd, a, _scan_masks())


def _rwkv_post_kernel(y_ref, bonus_ref, g_ref, w_ref, b_ref, bd_ref, o_ref):
    y = y_ref[0] + y_ref[1]
    bd = bd_ref[...]
    mu = _dot_exactb(y, bd) * (1.0 / RWKV_HEAD)
    yc = y - mu
    var = _dot_exactb(yc * yc, bd) * (1.0 / RWKV_HEAD)
    yn = yc * lax.rsqrt(var + GN_EPS) * w_ref[...] + b_ref[...]
    o_ref[...] = (yn + bonus_ref[...]) * g_ref[...]


def _rwkv_post(y2, bonus, g, lnx_w, lnx_b):
    S = bonus.shape[0]
    tm = min(512, S)
    W = RWKV_W
    row = pl.BlockSpec((tm, W), lambda i: (i, 0))
    return pl.pallas_call(
        _rwkv_post_kernel,
        grid=(S // tm,),
        in_specs=[pl.BlockSpec((2, tm, W), lambda i: (0, i, 0)), row, row,
                  pl.BlockSpec((1, W), lambda i: (0, 0)), pl.BlockSpec((1, W), lambda i: (0, 0)),
                  pl.BlockSpec((W, W), lambda i: (0, 0))],
        out_specs=row,
        out_shape=jax.ShapeDtypeStruct((S, W), F32),
        compiler_params=_cparams(("parallel",)),
        name="rwkv_post",
    )(y2, bonus, g, lnx_w.reshape(1, W), lnx_b.reshape(1, W), _head_blockdiag())


def _lru_kernel(rev, tm, nblk, x_ref, xp_ref, xn_ref, cw_ref, cb_ref, wa_ref, ba_ref, wx_ref, bx_ref, lam_ref,
                o_ref, carry_ref):
    c = pl.program_id(0)
    ib = nblk - 1 - c if rev else c

    @pl.when(c == 0)
    def _():
        carry_ref[...] = jnp.zeros_like(carry_ref)

    x = x_ref[...]
    rows = lax.broadcasted_iota(jnp.int32, (tm, 1), 0)
    prev_row = jnp.where(ib == 0, 0.0, xp_ref[SUBLANES - 1:SUBLANES, :])
    nx0 = jnp.where(ib == nblk - 1, 0.0, xn_ref[0:1, :])
    nx1 = jnp.where(ib == nblk - 1, 0.0, xn_ref[1:2, :])
    xm1 = jnp.where(rows == 0, prev_row, pltpu.roll(x, 1, 0))
    xp1 = jnp.where(rows == tm - 1, nx0, pltpu.roll(x, tm - 1, 0))
    xp2 = jnp.where(rows == tm - 2, nx0, jnp.where(rows == tm - 1, nx1, pltpu.roll(x, tm - 2, 0)))
    xc = (cw_ref[0:1, :] * xm1 + cw_ref[1:2, :] * x + cw_ref[2:3, :] * xp1 + cw_ref[3:4, :] * xp2) + cb_ref[...]
    xcb = xc.astype(BF16)
    ra, ia = [], []
    for g in range(LRU_BLOCKS):
        blk = xcb[:, g * LRU_BLOCK:(g + 1) * LRU_BLOCK]
        ra.append(_dot(blk, wa_ref[g]))
        ia.append(_dot(blk, wx_ref[g]))
    rg = _sigmoid(jnp.concatenate(ra, axis=1) + ba_ref[...])
    ig = _sigmoid(jnp.concatenate(ia, axis=1) + bx_ref[...])
    log_a = -LRU_C * rg * _softplus(-lam_ref[...])
    a = jnp.exp(log_a)
    b = jnp.sqrt(1.0 - jnp.exp(2.0 * log_a)) * (ig * xc)
    k = 1
    while k < tm:
        if rev:
            ap, bp = pltpu.roll(a, tm - k, 0), pltpu.roll(b, tm - k, 0)
            valid = rows < tm - k
        else:
            ap, bp = pltpu.roll(a, k, 0), pltpu.roll(b, k, 0)
            valid = rows >= k
        b = jnp.where(valid, a * bp + b, b)
        a = jnp.where(valid, a * ap, a)
        k *= 2
    h = a * carry_ref[0:1, :] + b
    o_ref[...] = h
    last = h[0:1, :] if rev else h[tm - 1:tm, :]
    carry_ref[...] = jnp.broadcast_to(last, carry_ref.shape)


def _lru_scan(h_lru, l, p, d):
    S = h_lru.shape[0]
    tm = min(512, S)
    nblk = S // tm
    W = LRU_W
    rev = d == 1
    hb = tm // SUBLANES
    nrow8 = S // SUBLANES
    blk = (lambda c: nblk - 1 - c) if rev else (lambda c: c)
    const = lambda shape: pl.BlockSpec(shape, lambda c: (0,) * len(shape))
    return pl.pallas_call(
        functools.partial(_lru_kernel, rev, tm, nblk),
        grid=(nblk,),
        in_specs=[
            pl.BlockSpec((tm, W), lambda c: (blk(c), 0)),
            pl.BlockSpec((SUBLANES, W), lambda c: (jnp.maximum(blk(c) * hb - 1, 0), 0)),
            pl.BlockSpec((SUBLANES, W), lambda c: (jnp.minimum((blk(c) + 1) * hb, nrow8 - 1), 0)),
            const((CONV_W, W)), const((1, W)),
            const((LRU_BLOCKS, LRU_BLOCK, LRU_BLOCK)), const((1, W)),
            const((LRU_BLOCKS, LRU_BLOCK, LRU_BLOCK)), const((1, W)), const((1, W)),
        ],
        out_specs=pl.BlockSpec((tm, W), lambda c: (blk(c), 0)),
        out_shape=jax.ShapeDtypeStruct((S, W), F32),
        scratch_shapes=[pltpu.VMEM((SUBLANES, W), F32)],
        compiler_params=_cparams(("arbitrary",)),
        name="lru_scan_rev" if rev else "lru_scan_fwd",
    )(h_lru, h_lru, h_lru, p['lru_conv_w'][l], p['lru_conv_b'][l].reshape(1, W),
      p['lru_wa'][l, d].astype(BF16), p['lru_ba'][l, d].reshape(1, W),
      p['lru_wx'][l, d].astype(BF16), p['lru_bx'][l, d].reshape(1, W), p['lru_lambda'][l, d].reshape(1, W))


MLA_QK = 2 * LANES


def _mla_prep_kernel(cq_ref, ckv_ref, kr_ref, pos_ref, invf_ref, qn_ref, kvn_ref, wq_ref, wk_ref, wv_ref,
                     q_o, k_o, v_o):
    def norm(x, g):
        ms = jnp.mean(x * x, axis=-1, keepdims=True)
        return (x * lax.rsqrt(ms + NORM_EPS) * g).astype(BF16)

    q = _dot(norm(cq_ref[...], qn_ref[...]), wq_ref[...])
    ckv = norm(ckv_ref[...], kvn_ref[...])
    kn = _dot(ckv, wk_ref[...])
    v = _dot(ckv, wv_ref[...])
    ang = pos_ref[...].astype(F32) * invf_ref[...]
    lane = lax.broadcasted_iota(jnp.int32, (1, LANES), 1)
    half = QK_ROPE // 2
    cos = jnp.cos(ang)
    sin = jnp.sin(ang)
    s_lo = jnp.where(lane < half, -sin, 0.0)
    s_hi = jnp.where((lane >= half) & (lane < QK_ROPE), sin, 0.0)

    def rope(t):
        return t * cos + pltpu.roll(t, LANES - half, 1) * s_lo + pltpu.roll(t, half, 1) * s_hi

    kpe = rope(kr_ref[...]).astype(BF16)
    for hh in range(MLA_HEADS):
        q_o[hh, :, 0:LANES] = q[:, hh * MLA_QK:hh * MLA_QK + LANES].astype(BF16)
        q_o[hh, :, LANES:] = rope(q[:, hh * MLA_QK + LANES:(hh + 1) * MLA_QK]).astype(BF16)
        k_o[hh, :, 0:LANES] = kn[:, hh * QK_NOPE:(hh + 1) * QK_NOPE].astype(BF16)
        k_o[hh, :, LANES:] = kpe
        v_o[hh] = v[:, hh * V_HEAD:(hh + 1) * V_HEAD].astype(BF16)


def _mla_prep(h_mla, positions, l, p):
    S = h_mla.shape[0]
    tm = min(512, S)
    half = QK_ROPE // 2
    inv = ROPE_THETA ** (-jnp.arange(0, QK_ROPE, 2, dtype=F32) / QK_ROPE)
    invf = jnp.zeros((1, LANES), F32).at[0, :half].set(inv).at[0, half:QK_ROPE].set(inv)
    wq = p['mla_w_uq'][l].reshape(Q_LORA, MLA_HEADS, QK_NOPE + QK_ROPE)
    wq = jnp.concatenate([wq, jnp.zeros((Q_LORA, MLA_HEADS, MLA_QK - QK_NOPE - QK_ROPE), F32)], axis=-1)
    wq = wq.reshape(Q_LORA, MLA_HEADS * MLA_QK).astype(BF16)
    const = lambda shape: pl.BlockSpec(shape, lambda i: (0,) * len(shape))
    hq = jax.ShapeDtypeStruct((MLA_HEADS, S, MLA_QK), BF16)
    return pl.pallas_call(
        _mla_prep_kernel,
        grid=(S // tm,),
        in_specs=[
            pl.BlockSpec((tm, Q_LORA), lambda i: (i, 0)),
            pl.BlockSpec((tm, KV_LORA), lambda i: (i, Q_LORA // KV_LORA)),
            pl.BlockSpec((tm, LANES), lambda i: (i, MLA_COL_KR // LANES)),
            pl.BlockSpec((tm, 1), lambda i: (i, 0)),
            const((1, LANES)), const((1, Q_LORA)), const((1, KV_LORA)),
            const((Q_LORA, MLA_HEADS * MLA_QK)), const((KV_LORA, MLA_HEADS * QK_NOPE)),
            const((KV_LORA, MLA_HEADS * V_HEAD)),
        ],
        out_specs=[pl.BlockSpec((MLA_HEADS, tm, MLA_QK), lambda i: (0, i, 0)),
                   pl.BlockSpec((MLA_HEADS, tm, MLA_QK), lambda i: (0, i, 0)),
                   pl.BlockSpec((MLA_HEADS, tm, V_HEAD), lambda i: (0, i, 0))],
        out_shape=[hq, hq, jax.ShapeDtypeStruct((MLA_HEADS, S, V_HEAD), BF16)],
        compiler_params=_cparams(("parallel",)),
        name="mla_prep",
    )(h_mla, h_mla, h_mla, positions.reshape(S, 1), invf, p['mla_q_norm'][l].reshape(1, Q_LORA),
      p['mla_kv_norm'][l].reshape(1, KV_LORA), wq, p['mla_w_uk'][l].astype(BF16), p['mla_w_uv'][l].astype(BF16))


def _flash_kernel(nkv, q_ref, k_ref, v_ref, o_ref, m_ref, l_ref, acc_ref):
    j = pl.program_id(2)

    @pl.when(j == 0)
    def _():
        m_ref[...] = jnp.full_like(m_ref, -jnp.inf)
        l_ref[...] = jnp.zeros_like(l_ref)
        acc_ref[...] = jnp.zeros_like(acc_ref)

    s = _dot_nt(q_ref[...], k_ref[...]) * ((QK_NOPE + QK_ROPE) ** -0.5)
    m_old = m_ref[...]
    m_new = jnp.maximum(m_old, jnp.max(s, axis=-1, keepdims=True))
    alpha = jnp.exp(m_old - m_new)
    pm = jnp.exp(s - m_new)
    l_ref[...] = alpha * l_ref[...] + jnp.sum(pm, axis=-1, keepdims=True)
    acc_ref[...] = alpha * acc_ref[...] + _dot(pm.astype(BF16), v_ref[...])
    m_ref[...] = m_new

    @pl.when(j == nkv - 1)
    def _():
        o_ref[...] = acc_ref[...] / l_ref[...]


def _flash_attention(q, k, v):
    S = q.shape[1]
    tq = min(512, S)
    tk = min(1024, S)
    nkv = S // tk
    return pl.pallas_call(
        functools.partial(_flash_kernel, nkv),
        grid=(MLA_HEADS, S // tq, nkv),
        in_specs=[pl.BlockSpec((None, tq, MLA_QK), lambda h, i, j: (h, i, 0)),
                  pl.BlockSpec((None, tk, MLA_QK), lambda h, i, j: (h, j, 0)),
                  pl.BlockSpec((None, tk, V_HEAD), lambda h, i, j: (h, j, 0))],
        out_specs=pl.BlockSpec((tq, V_HEAD), lambda h, i, j: (i, h)),
        out_shape=jax.ShapeDtypeStruct((S, MLA_W), F32),
        scratch_shapes=[pltpu.VMEM((tq, 1), F32), pltpu.VMEM((tq, 1), F32), pltpu.VMEM((tq, V_HEAD), F32)],
        compiler_params=_cparams(("parallel", "parallel", "arbitrary")),
        name="mla_flash",
    )(q, k, v)


def _merge_kernel(ya_ref, h0_ref, h1_ref, gb_ref, yc_ref, g0_ref, g1_ref, g2_ref, wa_ref, wb_ref, wc_ref, o_ref):
    yb = (h0_ref[...] + h1_ref[...]) * jax.nn.gelu(gb_ref[...])
    m = _sigmoid(g0_ref[...]) * _dot(ya_ref[...].astype(BF16), wa_ref[...])
    m = m + _sigmoid(g1_ref[...]) * _dot(yb.astype(BF16), wb_ref[...])
    m = m + _sigmoid(g2_ref[...]) * _dot(yc_ref[...].astype(BF16), wc_ref[...])
    o_ref[...] = m.astype(o_ref.dtype)


def _merge(ya, hs0, hs1, h_lru, yc, gates, l, p):
    S = ya.shape[0]
    tm = min(512, S)
    tn = 512
    nj = D_MODEL // tn
    rowa = pl.BlockSpec((tm, RWKV_W), lambda j, i: (i, 0))
    gate = lambda b: pl.BlockSpec((tm, tn), lambda j, i: (i, b * nj + j))
    wcol = lambda k: pl.BlockSpec((k, tn), lambda j, i: (0, j))
    return pl.pallas_call(
        _merge_kernel,
        grid=(nj, S // tm),
        in_specs=[rowa, rowa, rowa, pl.BlockSpec((tm, LRU_W), lambda j, i: (i, 1)),
                  pl.BlockSpec((tm, MLA_W), lambda j, i: (i, 0)),
                  gate(0), gate(1), gate(2), wcol(RWKV_W), wcol(LRU_W), wcol(MLA_W)],
        out_specs=pl.BlockSpec((tm, tn), lambda j, i: (i, j)),
        out_shape=jax.ShapeDtypeStruct((S, D_MODEL), BF16),
        compiler_params=_cparams(("parallel", "parallel")),
        name="merge",
    )(ya, hs0, hs1, h_lru, yc, gates, gates, gates,
      p['wo_rwkv'][l].astype(BF16), p['wo_lru'][l].astype(BF16), p['wo_mla'][l].astype(BF16))


def _router_kernel(x_ref, g_ref, wr_ref, xn_o, lg_o):
    x = x_ref[...]
    ms = jnp.mean(x * x, axis=-1, keepdims=True)
    xn = x * lax.rsqrt(ms + NORM_EPS) * g_ref[...]
    xn_o[...] = xn
    lg_o[...] = _dot3_nt(wr_ref[...], xn)


def _router(x, gain, w_router):
    S = x.shape[0]
    tm = min(512, S)
    return pl.pallas_call(
        _router_kernel,
        grid=(S // tm,),
        in_specs=[pl.BlockSpec((tm, D_MODEL), lambda i: (i, 0)), pl.BlockSpec((1, D_MODEL), lambda i: (0, 0)),
                  pl.BlockSpec((N_EXPERTS, D_MODEL), lambda i: (0, 0))],
        out_specs=[pl.BlockSpec((tm, D_MODEL), lambda i: (i, 0)), pl.BlockSpec((N_EXPERTS, tm), lambda i: (0, i))],
        out_shape=[jax.ShapeDtypeStruct((S, D_MODEL), F32), jax.ShapeDtypeStruct((N_EXPERTS, S), F32)],
        compiler_params=_cparams(("parallel",)),
        name="moe_router",
    )(x, gain.reshape(1, D_MODEL), w_router.T)


def _route_kernel(nb, cap, lg_ref, idx_o, gate_o, aff_ref):
    E = N_EXPERTS
    m = lg_ref[0]
    for e in range(1, E):
        m = jnp.maximum(m, lg_ref[e])
    z = jnp.zeros_like(m)
    for e in range(E):
        z = z + jnp.exp(lg_ref[e] - m)
    for e in range(E):
        aff_ref[e] = jnp.exp(lg_ref[e] - m) / z

    li = lax.broadcasted_iota(jnp.int32, (LANES, LANES), 0)
    lj = lax.broadcasted_iota(jnp.int32, (LANES, LANES), 1)
    upper = jnp.where(li <= lj, 1.0, 0.0).astype(BF16)
    bi = lax.broadcasted_iota(jnp.int32, (nb, nb), 0)
    bj = lax.broadcasted_iota(jnp.int32, (nb, nb), 1)
    lstrict = jnp.where(bj < bi, 1.0, 0.0).astype(BF16)
    qrow = lax.broadcasted_iota(jnp.int32, (1, cap), 1).astype(F32)
    brow = lax.broadcasted_iota(jnp.int32, (nb, cap), 0).astype(F32)
    irow = lax.broadcasted_iota(jnp.int32, (LANES, cap), 0).astype(F32)

    def prefix(mask):
        cum = _dot(mask.astype(BF16), upper)
        tot = cum[:, LANES - 1:LANES]
        off = _dot(lstrict, jnp.broadcast_to(tot, (nb, LANES)).astype(BF16))
        return cum, off, tot

    def expert(e, carry):
        aff = aff_ref[e]
        bits = pltpu.bitcast(aff, jnp.int32)

        def bisect(_, lohi):
            lo, hi = lohi
            mid = lo + lax.shift_right_arithmetic(hi - lo, 1)
            cnt = jnp.sum(jnp.where(bits >= mid, 1.0, 0.0), keepdims=True)
            ok = cnt >= cap
            return jnp.where(ok, mid, lo), jnp.where(ok, hi, mid)

        lo0 = jnp.zeros((1, 1), jnp.int32)
        hi0 = jnp.full((1, 1), 0x7F800000, jnp.int32)
        thr, _ = lax.fori_loop(0, 31, bisect, (lo0, hi0))
        gt = jnp.where(bits > thr, 1.0, 0.0)
        eq = jnp.where(bits == thr, 1.0, 0.0)
        need = cap - jnp.sum(gt, keepdims=True)
        ecum, eoff, _ = prefix(eq)
        sel = gt + eq * jnp.where(ecum + eoff <= need, 1.0, 0.0)
        cum, off, tot = prefix(sel)
        full = jnp.where(off[:, 0:1] + tot <= qrow, 1.0, 0.0)
        nfull = jnp.sum(full, axis=0, keepdims=True)
        offsel = jnp.sum(full * tot, axis=0, keepdims=True)
        onehot_b = jnp.where(brow == nfull, 1.0, 0.0).astype(BF16)
        cum_t = _dot_tn(cum.astype(BF16), onehot_b)
        within = jnp.sum(jnp.where(cum_t <= qrow - offsel, 1.0, 0.0), axis=0, keepdims=True)
        idx_o[e] = (nfull * LANES + within).astype(jnp.int32)
        a1 = aff.astype(BF16)
        r1 = aff - a1.astype(F32)
        a2 = r1.astype(BF16)
        a3 = (r1 - a2.astype(F32)).astype(BF16)
        aff_t = _dot_tn(a1, onehot_b) + _dot_tn(a2, onehot_b) + _dot_tn(a3, onehot_b)
        gate_o[e] = jnp.sum(jnp.where(irow == within, aff_t, 0.0), axis=0, keepdims=True)
        return carry

    lax.fori_loop(0, E, expert, 0)


def _route(logits_t, cap):
    E, S = logits_t.shape
    nb = S // LANES
    return pl.pallas_call(
        functools.partial(_route_kernel, nb, cap),
        out_shape=[jax.ShapeDtypeStruct((E, 1, cap), jnp.int32), jax.ShapeDtypeStruct((E, 1, cap), F32)],
        scratch_shapes=[pltpu.VMEM((E, nb, LANES), F32)],
        compiler_params=pltpu.CompilerParams(vmem_limit_bytes=VMEM_LIMIT),
        name="moe_route",
    )(logits_t.reshape(E, nb, LANES))


def _expert_kernel(tq, nt, idx_ref, xacc_hbm, xn_hbm, gate_ref, wg_ref, wu_ref, wd_ref, out_hbm, xs_ref, ob_ref, sems):
    del xacc_hbm
    e = pl.program_id(0)
    t = pl.program_id(1)
    base = (e * nt + t) * tq

    def gather_copies(r):
        tok = idx_ref[base + r]
        return (pltpu.make_async_copy(xn_hbm.at[pl.ds(tok, 1), :], xs_ref.at[pl.ds(r, 1), :], sems.at[0]),
                pltpu.make_async_copy(out_hbm.at[pl.ds(tok, 1), :], ob_ref.at[pl.ds(r, 1), :], sems.at[1]))

    def scatter_copy(r):
        tok = idx_ref[base + r]
        return pltpu.make_async_copy(ob_ref.at[pl.ds(r, 1), :], out_hbm.at[pl.ds(tok, 1), :], sems.at[2])

    def start_gather(r, c):
        gx, go = gather_copies(r)
        gx.start()
        go.start()
        return c

    def wait_gather(r, c):
        gx, go = gather_copies(r)
        gx.wait()
        go.wait()
        return c

    lax.fori_loop(0, tq, start_gather, 0)
    lax.fori_loop(0, tq, wait_gather, 0)
    xs = xs_ref[...].astype(BF16)
    hg = _dot(xs, wg_ref[...])
    hu = _dot(xs, wu_ref[...])
    hdn = (hg * _sigmoid(hg)) * hu
    ye = _dot(hdn.astype(BF16), wd_ref[...])
    ob_ref[...] = ob_ref[...] + ye * gate_ref[...]

    def start_scatter(r, c):
        scatter_copy(r).start()
        return c

    def wait_scatter(r, c):
        scatter_copy(r).wait()
        return c

    lax.fori_loop(0, tq, start_scatter, 0)
    lax.fori_loop(0, tq, wait_scatter, 0)


def _experts(x, xn, idx, gate, l, p):
    S = x.shape[0]
    E, cap = idx.shape
    tq = min(256, cap)
    nt = cap // tq
    grid_spec = pltpu.PrefetchScalarGridSpec(
        num_scalar_prefetch=1,
        grid=(E, nt),
        in_specs=[
            pl.BlockSpec(memory_space=pl.ANY),
            pl.BlockSpec(memory_space=pl.ANY),
            pl.BlockSpec((None, tq, 1), lambda e, t, idx: (e, t, 0)),
            pl.BlockSpec((None, D_MODEL, EXPERT_FF), lambda e, t, idx: (e, 0, 0)),
            pl.BlockSpec((None, D_MODEL, EXPERT_FF), lambda e, t, idx: (e, 0, 0)),
            pl.BlockSpec((None, EXPERT_FF, D_MODEL), lambda e, t, idx: (e, 0, 0)),
        ],
        out_specs=pl.BlockSpec(memory_space=pl.ANY),
        scratch_shapes=[pltpu.VMEM((tq, D_MODEL), F32), pltpu.VMEM((tq, D_MODEL), F32),
                        pltpu.SemaphoreType.DMA((3,))],
    )
    return pl.pallas_call(
        functools.partial(_expert_kernel, tq, nt),
        grid_spec=grid_spec,
        out_shape=jax.ShapeDtypeStruct((S, D_MODEL), F32),
        input_output_aliases={1: 0},
        compiler_params=_cparams(("arbitrary", "arbitrary")),
        name="moe_experts",
    )(idx.reshape(E * cap), x, xn, gate.reshape(E, cap, 1),
      p['w_gate'][l].astype(BF16), p['w_up'][l].astype(BF16), p['w_down'][l].astype(BF16))


def _trunk(x, positions, p):
    S = x.shape[0]
    cap = CAPACITY_FACTOR * S // N_EXPERTS
    depth = p['w_in'].shape[0]
    v_first = None
    for l in range(depth):
        xn = _rmsnorm(x, p['norm_mix'][l], BF16)
        h_rwkv, h_lru, h_mla, gates = _in_projection(xn, p['w_in'][l], p['rwkv_vres_down'][l - 1] if l > 0 else None)
        r, v, kk, lw, kd, a, g, bonus = _rwkv_prep(h_rwkv, h_mla, l, p, v_first)
        if l == 0:
            v_first = v
        y2 = _rwkv_scan(r, v, kk, lw, kd, a)
        ya = _rwkv_post(y2, bonus, g, p['rwkv_lnx_w'][l], p['rwkv_lnx_b'][l])
        hs0 = _lru_scan(h_lru, l, p, 0)
        hs1 = _lru_scan(h_lru, l, p, 1)
        q, k, vv = _mla_prep(h_mla, positions, l, p)
        yc = _flash_attention(q, k, vv)
        merged = _merge(ya, hs0, hs1, h_lru, yc, gates, l, p)
        x = _matmul(merged, p['w_out'][l].astype(BF16), D_MODEL, res=x, name="outproj")
        xn2, logits_t = _router(x, p['norm_ffn'][l], p['w_router'][l])
        idx, gate = _route(logits_t, cap)
        x = _experts(x, xn2, idx.reshape(N_EXPERTS, cap), gate.reshape(N_EXPERTS, cap), l, p)
    return _rmsnorm(x, p['norm_final'], F32)


def kernel(x, positions, norm_mix, w_in, rwkv_shift_mu, rwkv_w0, rwkv_w2, rwkv_a0, rwkv_a2, rwkv_g2, rwkv_k_k,
           rwkv_k_a, rwkv_r_k, rwkv_lnx_w, rwkv_lnx_b, rwkv_vres_down, rwkv_vres_up, rwkv_vres_b, lru_conv_w,
           lru_conv_b, lru_wa, lru_ba, lru_wx, lru_bx, lru_lambda, mla_q_norm, mla_kv_norm, mla_w_uq, mla_w_uk,
           mla_w_uv, wo_rwkv, wo_lru, wo_mla, w_out, norm_ffn, w_router, w_gate, w_up, w_down, norm_final):
    p = dict(norm_mix=norm_mix, w_in=w_in, rwkv_shift_mu=rwkv_shift_mu, rwkv_w0=rwkv_w0, rwkv_w2=rwkv_w2,
             rwkv_a0=rwkv_a0, rwkv_a2=rwkv_a2, rwkv_g2=rwkv_g2, rwkv_k_k=rwkv_k_k, rwkv_k_a=rwkv_k_a,
             rwkv_r_k=rwkv_r_k, rwkv_lnx_w=rwkv_lnx_w, rwkv_lnx_b=rwkv_lnx_b, rwkv_vres_down=rwkv_vres_down,
             rwkv_vres_up=rwkv_vres_up, rwkv_vres_b=rwkv_vres_b, lru_conv_w=lru_conv_w, lru_conv_b=lru_conv_b,
             lru_wa=lru_wa, lru_ba=lru_ba, lru_wx=lru_wx, lru_bx=lru_bx, lru_lambda=lru_lambda,
             mla_q_norm=mla_q_norm, mla_kv_norm=mla_kv_norm, mla_w_uq=mla_w_uq, mla_w_uk=mla_w_uk,
             mla_w_uv=mla_w_uv, wo_rwkv=wo_rwkv, wo_lru=wo_lru, wo_mla=wo_mla, w_out=w_out, norm_ffn=norm_ffn,
             w_router=w_router, w_gate=w_gate, w_up=w_up, w_down=w_down, norm_final=norm_final)
    B, S, D = x.shape
    outs = [_trunk(x[b], positions[b], p) for b in range(B)]
    return jnp.stack(outs, axis=0)
```

```python
import functools

import numpy as np
import jax
import jax.numpy as jnp
from jax import lax
from jax.experimental import pallas as pl
from jax.experimental.pallas import tpu as pltpu

F32 = jnp.float32
BF16 = jnp.bfloat16

D_MODEL = 2048
RWKV_HEADS = 12
RWKV_HEAD = 64
RWKV_W = RWKV_HEADS * RWKV_HEAD
DECAY_LORA = 64
ICLR_LORA = 64
VRES_LORA = 32
GATE_LORA = 128
GN_EPS = 64e-5
LRU_BLOCKS = 6
LRU_BLOCK = 128
LRU_W = LRU_BLOCKS * LRU_BLOCK
CONV_W = 4
LRU_C = 8.0
MLA_HEADS = 4
QK_NOPE = 128
QK_ROPE = 64
V_HEAD = 128
Q_LORA = 384
KV_LORA = 128
MLA_W = MLA_HEADS * V_HEAD
ROPE_THETA = 10000.0
N_BRANCH = 3
N_EXPERTS = 16
EXPERT_FF = 1024
CAPACITY_FACTOR = 2
NORM_EPS = 1e-6
RWKV_COLS = 3 * RWKV_W + 2 * DECAY_LORA + 2 * ICLR_LORA + GATE_LORA

LANES = 128
SUBLANES = 8
VMEM_LIMIT = 56 * 1024 * 1024

SEG_LRU = RWKV_COLS
SEG_MLA = SEG_LRU + 2 * LRU_W
SEG_GATES = SEG_MLA + Q_LORA + KV_LORA + QK_ROPE
MLA_SEG_W = Q_LORA + KV_LORA + 2 * LANES
MLA_COL_KR = Q_LORA + KV_LORA
MLA_COL_VRES = MLA_COL_KR + LANES

CHUNK = 64
PAIR = 2 * RWKV_HEAD


def _cparams(sem):
    return pltpu.CompilerParams(dimension_semantics=sem, vmem_limit_bytes=VMEM_LIMIT)


def _dot(a, b):
    return jnp.dot(a, b, preferred_element_type=F32)


def _dot_nt(a, b):
    return lax.dot_general(a, b, (((1,), (1,)), ((), ())), preferred_element_type=F32)


def _dot_tn(a, b):
    return lax.dot_general(a, b, (((0,), (0,)), ((), ())), preferred_element_type=F32)


def _split(x):
    hi = x.astype(BF16)
    lo = (x - hi.astype(F32)).astype(BF16)
    return hi, lo


def _dot_exactb(x, b):
    hi, lo = _split(x)
    return _dot(hi, b) + _dot(lo, b)


def _dot3(a, b):
    ah, al = _split(a)
    bh, bl = _split(b)
    return _dot(ah, bh) + (_dot(ah, bl) + _dot(al, bh))


def _dot3_nt(a, b):
    ah, al = _split(a)
    bh, bl = _split(b)
    return _dot_nt(ah, bh) + (_dot_nt(ah, bl) + _dot_nt(al, bh))


def _dot3_tn(a, b):
    ah, al = _split(a)
    bh, bl = _split(b)
    return _dot_tn(ah, bh) + (_dot_tn(ah, bl) + _dot_tn(al, bh))


def _sigmoid(x):
    return 1.0 / (1.0 + jnp.exp(-x))


def _softplus(x):
    return jnp.maximum(x, 0.0) + jnp.log(1.0 + jnp.exp(-jnp.abs(x)))


def _rmsnorm_kernel(x_ref, g_ref, o_ref):
    x = x_ref[...]
    ms = jnp.mean(x * x, axis=-1, keepdims=True)
    o_ref[...] = (x * lax.rsqrt(ms + NORM_EPS) * g_ref[...]).astype(o_ref.dtype)


def _rmsnorm(x, gain, dtype):
    S, D = x.shape
    tm = min(512, S)
    return pl.pallas_call(
        _rmsnorm_kernel,
        grid=(S // tm,),
        in_specs=[pl.BlockSpec((tm, D), lambda i: (i, 0)), pl.BlockSpec((1, D), lambda i: (0, 0))],
        out_specs=pl.BlockSpec((tm, D), lambda i: (i, 0)),
        out_shape=jax.ShapeDtypeStruct((S, D), dtype),
        compiler_params=_cparams(("parallel",)),
        name="rmsnorm",
    )(x, gain.reshape(1, D))


def _matmul_kernel(has_res, *refs):
    if has_res:
        a_ref, w_ref, res_ref, o_ref = refs
        o_ref[...] = res_ref[...] + _dot(a_ref[...], w_ref[...])
    else:
        a_ref, w_ref, o_ref = refs
        o_ref[...] = _dot(a_ref[...], w_ref[...])


def _matmul(a, w, tn, res=None, name="matmul"):
    M, K = a.shape
    N = w.shape[1]
    tm = min(512, M)
    in_specs = [pl.BlockSpec((tm, K), lambda j, i: (i, 0)), pl.BlockSpec((K, tn), lambda j, i: (0, j))]
    args = [a, w]
    if res is not None:
        in_specs.append(pl.BlockSpec((tm, tn), lambda j, i: (i, j)))
        args.append(res)
    return pl.pallas_call(
        functools.partial(_matmul_kernel, res is not None),
        grid=(N // tn, M // tm),
        in_specs=in_specs,
        out_specs=pl.BlockSpec((tm, tn), lambda j, i: (i, j)),
        out_shape=jax.ShapeDtypeStruct((M, N), F32),
        compiler_params=_cparams(("parallel", "parallel")),
        name=name,
    )(*args)


def _in_projection(xn, w_in, vres_down):
    w = w_in.astype(BF16)
    w_mla = jnp.zeros((D_MODEL, MLA_SEG_W), BF16)
    w_mla = w_mla.at[:, :MLA_COL_KR + QK_ROPE].set(w[:, SEG_MLA:SEG_GATES])
    if vres_down is not None:
        w_mla = w_mla.at[:, MLA_COL_VRES:MLA_COL_VRES + VRES_LORA].set(vres_down.astype(BF16))
    h_rwkv = _matmul(xn, w[:, :SEG_LRU], RWKV_COLS, name="inproj_rwkv")
    h_lru = _matmul(xn, w[:, SEG_LRU:SEG_MLA], 2 * LRU_W, name="inproj_lru")
    h_mla = _matmul(xn, w_mla, MLA_SEG_W, name="inproj_mla")
    gates = _matmul(xn, w[:, SEG_GATES:], D_MODEL, name="inproj_gates")
    return h_rwkv, h_lru, h_mla, gates


def _rwkv_prep_kernel(has_vres, tm, nblk, *refs):
    if has_vres:
        (h_ref, hp_ref, hn_ref, lora_ref, vfirst_ref, vup_ref, vb_ref, *rest) = refs
    else:
        (h_ref, hp_ref, hn_ref, *rest) = refs
    (mu_ref, w0_ref, w2_ref, a0_ref, a2_ref, g2_ref, kkw_ref, ka_ref, rk_ref, bd_ref,
     r_o, v_o, kk_o, lw_o, kd_o, a_o, g_o, bonus_o) = rest
    i = pl.program_id(0)
    p = h_ref[...]
    rows = lax.broadcasted_iota(jnp.int32, (tm, 1), 0)
    prev_row = jnp.where(i == 0, 0.0, hp_ref[SUBLANES - 1:SUBLANES, :])
    next_row = jnp.where(i == nblk - 1, 0.0, hn_ref[0:1, :])
    prev = jnp.where(rows == 0, prev_row, pltpu.roll(p, 1, 0))
    nxt = jnp.where(rows == tm - 1, next_row, pltpu.roll(p, tm - 1, 0))
    hs = p + mu_ref[0:1, :] * (prev - p) + mu_ref[1:2, :] * (nxt - p)

    W = RWKV_W
    r = hs[:, 0:W]
    k = hs[:, W:2 * W]
    v = hs[:, 2 * W:3 * W]
    wd = hs[:, 3 * W:3 * W + LANES]
    ad = hs[:, 3 * W + LANES:3 * W + 2 * LANES]
    gd = hs[:, 3 * W + 2 * LANES:3 * W + 3 * LANES]
    if has_vres:
        vg = _sigmoid(vb_ref[...] + _dot(lora_ref[...].astype(BF16), vup_ref[...]))
        v = v + (vfirst_ref[...] - v) * vg
    bd = bd_ref[...]
    kk = k * kkw_ref[...]
    n2 = _dot_exactb(kk * kk, bd)
    kk = kk / jnp.maximum(jnp.sqrt(n2), 1e-12)
    tw = jnp.tanh(wd).astype(BF16)
    adb = ad.astype(BF16)
    ksum = None
    for d in range(2):
        wl = w0_ref[d:d + 1, :] + _dot(tw, w2_ref[d])
        w_log = -_softplus(-wl) - 0.5
        lw_o[d] = -jnp.exp(w_log)
        a = _sigmoid(a0_ref[d:d + 1, :] + _dot(adb, a2_ref[d]))
        kd = k * (1.0 + (a - 1.0) * ka_ref[...])
        a_o[d] = a
        kd_o[d] = kd
        ksum = kd if ksum is None else ksum + kd
    r_o[...] = r
    v_o[...] = v
    kk_o[...] = kk
    g_o[...] = _dot(_sigmoid(gd).astype(BF16), g2_ref[...])
    bonus_o[...] = _dot_exactb(r * ksum * rk_ref[...], bd) * v


def _head_blockdiag():
    hid = np.arange(RWKV_W) // RWKV_HEAD
    return jnp.asarray((hid[:, None] == hid[None, :]).astype(np.float32), dtype=BF16)


def _rwkv_prep(h, h_mla, l, p, v_first):
    S = h.shape[0]
    tm = min(256, S)
    nblk = S // tm
    has_vres = l > 0
    W = RWKV_W
    hb = tm // SUBLANES
    nrow8 = S // SUBLANES

    def zpad(w, rows_before):
        z = jnp.zeros((LANES, W), F32)
        return z.at[rows_before:rows_before + w.shape[0]].set(w)

    w2 = jnp.stack([zpad(p['rwkv_w2'][l, 0], 0), zpad(p['rwkv_w2'][l, 1], DECAY_LORA)]).astype(BF16)
    a2 = jnp.stack([zpad(p['rwkv_a2'][l, 0], 0), zpad(p['rwkv_a2'][l, 1], ICLR_LORA)]).astype(BF16)
    const2 = lambda shape: pl.BlockSpec(shape, lambda i: (0,) * len(shape))
    in_specs = [
        pl.BlockSpec((tm, RWKV_COLS), lambda i: (i, 0)),
        pl.BlockSpec((SUBLANES, RWKV_COLS), lambda i: (jnp.maximum(i * hb - 1, 0), 0)),
        pl.BlockSpec((SUBLANES, RWKV_COLS), lambda i: (jnp.minimum((i + 1) * hb, nrow8 - 1), 0)),
    ]
    args = [h, h, h]
    if has_vres:
        in_specs += [
            pl.BlockSpec((tm, LANES), lambda i: (i, MLA_COL_VRES // LANES)),
            pl.BlockSpec((tm, W), lambda i: (i, 0)),
            const2((LANES, W)),
            const2((1, W)),
        ]
        vup = jnp.zeros((LANES, W), F32).at[:VRES_LORA].set(p['rwkv_vres_up'][l - 1]).astype(BF16)
        args += [h_mla, v_first, vup, p['rwkv_vres_b'][l - 1].reshape(1, W)]
    in_specs += [
        const2((2, RWKV_COLS)), const2((2, W)), const2((2, LANES, W)), const2((2, W)), const2((2, LANES, W)),
        const2((LANES, W)), const2((1, W)), const2((1, W)), const2((1, W)), const2((W, W)),
    ]
    args += [
        p['rwkv_shift_mu'][l], p['rwkv_w0'][l], w2, p['rwkv_a0'][l], a2,
        p['rwkv_g2'][l].astype(BF16), p['rwkv_k_k'][l].reshape(1, W), p['rwkv_k_a'][l].reshape(1, W),
        p['rwkv_r_k'][l].reshape(1, W), _head_blockdiag(),
    ]
    row = pl.BlockSpec((tm, W), lambda i: (i, 0))
    row2 = pl.BlockSpec((2, tm, W), lambda i: (0, i, 0))
    sds = jax.ShapeDtypeStruct((S, W), F32)
    sds2 = jax.ShapeDtypeStruct((2, S, W), F32)
    return pl.pallas_call(
        functools.partial(_rwkv_prep_kernel, has_vres, tm, nblk),
        grid=(nblk,),
        in_specs=in_specs,
        out_specs=[row, row, row, row2, row2, row2, row, row],
        out_shape=[sds, sds, sds, sds2, sds2, sds2, sds, sds],
        compiler_params=_cparams(("parallel",)),
        name="rwkv_prep",
    )(*args)


N_MASK = 8


def _scan_masks():
    n = PAIR
    out = np.zeros((2, N_MASK, n, n), np.float32)
    r = np.arange(n)[:, None]
    c = np.arange(n)[None, :]
    same = (r // CHUNK) == (c // CHUNK)
    for d in range(2):
        before = (c < r) if d == 0 else (c > r)
        strict = same & before
        out[d, 0] = strict
        out[d, 1] = same & (before | (c == r))
        out[d, 2] = strict & ((r // 8) == (c // 8))
        for j, b in enumerate((8, 16, 32)):
            out[d, 3 + j] = strict & ((r // (2 * b)) == (c // (2 * b))) & ((r // b) != (c // b))
        out[d, 6] = (r == c)
        out[d, 7] = (before | (c == r)) & (r < CHUNK) & (c < CHUNK)
    return jnp.asarray(out)


def _mm(a, b):
    return _dot(a.astype(BF16), b.astype(BF16))


def _mm_nt(a, b):
    return _dot_nt(a.astype(BF16), b.astype(BF16))


def _mm_tn(a, b):
    return _dot_tn(a.astype(BF16), b.astype(BF16))


def _rwkv_scan_kernel(nsub, rf, vf, kkf, lwf, kf, af, rb, vb, kkb, lwb, kb_, ab, m_ref, yf, yb, s_ref):
    T = CHUNK
    dirs = ((rf, vf, kkf, lwf, kf, af, yf), (rb, vb, kkb, lwb, kb_, ab, yb))

    @pl.when(pl.program_id(1) == 0)
    def _():
        s_ref[...] = jnp.zeros_like(s_ref)

    lane = lax.broadcasted_iota(jnp.int32, (1, PAIR), 1)
    m0 = jnp.where(lane < RWKV_HEAD, 1.0, 0.0)
    m1 = 1.0 - m0

    def stack(x):
        return jnp.concatenate([x * m0, x * m1], axis=0)

    def local(d, j):
        r_ref, v_ref, kk_ref, lw_ref, k_ref, a_ref, _ = dirs[d]
        sl = slice(j * T, (j + 1) * T)
        strict, incl, eye = m_ref[d, 0], m_ref[d, 1], m_ref[d, 6]
        lw = lw_ref[sl, :]
        kk = kk_ref[sl, :]
        beta = a_ref[sl, :] * kk
        kd = k_ref[sl, :]
        cs = m_ref[d, 7][0:T, 0:T].astype(BF16)
        lh, ll = _split(lw)
        c = _dot(cs, lh) + _dot(cs, ll)
        c_tot = c[0:1, :] if d == 1 else c[T - 1:T, :]
        e_inv = jnp.exp(-c)
        e_fin = jnp.exp(c_tot - c)
        kt = stack(kk * jnp.exp(c - lw))
        rt = stack(r_ref[sl, :] * jnp.exp(c))
        bb = stack(beta * e_inv)
        kb = stack(kd * e_inv)
        bh = stack(beta * e_fin)
        kh = stack(kd * e_fin)
        vs = stack(v_ref[sl, :])

        yield
        p1 = _mm_nt(jnp.concatenate([kt, rt], axis=0), jnp.concatenate([bb, kb], axis=0))
        A = p1[0:PAIR, 0:PAIR] * strict
        B = p1[0:PAIR, PAIR:] * strict
        C = p1[PAIR:, 0:PAIR] * incl
        E = p1[PAIR:, PAIR:] * incl
        yield

        d8 = A * m_ref[d, 2]
        d2 = _mm(d8, d8)
        bv = _mm(B, vs)
        yield
        d4 = _mm(d2, d2)
        x = _mm(eye - d8, eye + d2)
        yield
        x = _mm(x, eye + d4)
        yield
        for lvl in range(3):
            aoff = A * m_ref[d, 3 + lvl]
            ax = _mm(aoff, x)
            yield
            x = x - _mm(x, ax)
            yield

        mu = -_mm(x, jnp.concatenate([kt, bv], axis=1))
        ev = _mm(E, vs)
        hv = _mm_tn(kh, vs)
        yield
        cm = _mm(C, mu)
        gh = _mm_tn(bh, mu)
        yield
        rm = rt + cm[:, 0:PAIR]
        y0 = cm[:, PAIR:] + ev
        G = gh[:, 0:PAIR] + eye * jnp.exp(c_tot)
        H = gh[:, PAIR:] + hv
        return rm, y0, G, H

    keys = [(d, j) for d in range(2) for j in range(nsub)]
    gens = [local(d, j) for d, j in keys]
    loc = {}
    while len(loc) < len(keys):
        for key, gen in zip(keys, gens):
            try:
                next(gen)
            except StopIteration as done:
                loc[key] = done.value
    s = [s_ref[0], s_ref[1]]
    for step in range(nsub):
        for d in range(2):
            j = step if d == 0 else nsub - 1 - step
            rm, y0, G, H = loc[d, j]
            yst = _dot3(rm, s[d]) + y0
            dirs[d][6][j * T:(j + 1) * T, :] = yst[0:T, :] + yst[T:, :]
            s[d] = _dot3(G, s[d]) + H
    s_ref[0] = s[0]
    s_ref[1] = s[1]


def _rwkv_scan(r, v, kk, lw, kd, a):
    S = r.shape[0]
    tb = min(256, S)
    nsub = tb // CHUNK
    nblk = S // tb
    npair = RWKV_W // PAIR
    fwd = pl.BlockSpec((tb, PAIR), lambda pr, c: (c, pr))
    bwd = pl.BlockSpec((tb, PAIR), lambda pr, c: (nblk - 1 - c, pr))
    fwd2 = pl.BlockSpec((None, tb, PAIR), lambda pr, c: (0, c, pr))
    bwd2 = pl.BlockSpec((None, tb, PAIR), lambda pr, c: (1, nblk - 1 - c, pr))
    sds = jax.ShapeDtypeStruct((S, RWKV_W), F32)
    return pl.pallas_call(
        functools.partial(_rwkv_scan_kernel, nsub),
        grid=(npair, nblk),
        in_specs=[fwd, fwd, fwd, fwd2, fwd2, fwd2, bwd, bwd, bwd, bwd2, bwd2, bwd2,
                  pl.BlockSpec((2, N_MASK, PAIR, PAIR), lambda pr, c: (0, 0, 0, 0))],
        out_specs=[fwd, bwd],
        out_shape=[sds, sds],
        scratch_shapes=[pltpu.VMEM((2, PAIR, PAIR), F32)],
        compiler_params=_cparams(("parallel", "arbitrary")),
        name="rwkv_scan",
    )(r, v, kk, lw, kd, a, r, v, kk, lw, kd, a, _scan_masks())


def _rwkv_post_kernel(yf_ref, yb_ref, bonus_ref, g_ref, w_ref, b_ref, bd_ref, o_ref):
    y = yf_ref[...] + yb_ref[...]
    bd = bd_ref[...]
    mu = _dot_exactb(y, bd) * (1.0 / RWKV_HEAD)
    yc = y - mu
    var = _dot_exactb(yc * yc, bd) * (1.0 / RWKV_HEAD)
    yn = yc * lax.rsqrt(var + GN_EPS) * w_ref[...] + b_ref[...]
    o_ref[...] = (yn + bonus_ref[...]) * g_ref[...]


def _rwkv_post(yf, yb, bonus, g, lnx_w, lnx_b):
    S = bonus.shape[0]
    tm = min(512, S)
    W = RWKV_W
    row = pl.BlockSpec((tm, W), lambda i: (i, 0))
    return pl.pallas_call(
        _rwkv_post_kernel,
        grid=(S // tm,),
        in_specs=[row, row, row, row,
                  pl.BlockSpec((1, W), lambda i: (0, 0)), pl.BlockSpec((1, W), lambda i: (0, 0)),
                  pl.BlockSpec((W, W), lambda i: (0, 0))],
        out_specs=row,
        out_shape=jax.ShapeDtypeStruct((S, W), F32),
        compiler_params=_cparams(("parallel",)),
        name="rwkv_post",
    )(yf, yb, bonus, g, lnx_w.reshape(1, W), lnx_b.reshape(1, W), _head_blockdiag())


def _lru_kernel(rev, tm, nblk, x_ref, xp_ref, xn_ref, cw_ref, cb_ref, wa_ref, ba_ref, wx_ref, bx_ref, lam_ref,
                o_ref, carry_ref):
    c = pl.program_id(0)
    ib = nblk - 1 - c if rev else c

    @pl.when(c == 0)
    def _():
        carry_ref[...] = jnp.zeros_like(carry_ref)

    x = x_ref[...]
    rows = lax.broadcasted_iota(jnp.int32, (tm, 1), 0)
    prev_row = jnp.where(ib == 0, 0.0, xp_ref[SUBLANES - 1:SUBLANES, :])
    nx0 = jnp.where(ib == nblk - 1, 0.0, xn_ref[0:1, :])
    nx1 = jnp.where(ib == nblk - 1, 0.0, xn_ref[1:2, :])
    xm1 = jnp.where(rows == 0, prev_row, pltpu.roll(x, 1, 0))
    xp1 = jnp.where(rows == tm - 1, nx0, pltpu.roll(x, tm - 1, 0))
    xp2 = jnp.where(rows == tm - 2, nx0, jnp.where(rows == tm - 1, nx1, pltpu.roll(x, tm - 2, 0)))
    xc = (cw_ref[0:1, :] * xm1 + cw_ref[1:2, :] * x + cw_ref[2:3, :] * xp1 + cw_ref[3:4, :] * xp2) + cb_ref[...]
    xcb = xc.astype(BF16)
    ra, ia = [], []
    for g in range(LRU_BLOCKS):
        blk = xcb[:, g * LRU_BLOCK:(g + 1) * LRU_BLOCK]
        ra.append(_dot(blk, wa_ref[g]))
        ia.append(_dot(blk, wx_ref[g]))
    rg = _sigmoid(jnp.concatenate(ra, axis=1) + ba_ref[...])
    ig = _sigmoid(jnp.concatenate(ia, axis=1) + bx_ref[...])
    log_a = -LRU_C * rg * _softplus(-lam_ref[...])
    a = jnp.exp(log_a)
    b = jnp.sqrt(1.0 - jnp.exp(2.0 * log_a)) * (ig * xc)
    k = 1
    while k < tm:
        if rev:
            ap, bp = pltpu.roll(a, tm - k, 0), pltpu.roll(b, tm - k, 0)
            valid = rows < tm - k
        else:
            ap, bp = pltpu.roll(a, k, 0), pltpu.roll(b, k, 0)
            valid = rows >= k
        b = jnp.where(valid, a * bp + b, b)
        a = jnp.where(valid, a * ap, a)
        k *= 2
    h = a * carry_ref[0:1, :] + b
    o_ref[...] = h
    last = h[0:1, :] if rev else h[tm - 1:tm, :]
    carry_ref[...] = jnp.broadcast_to(last, carry_ref.shape)


def _lru_scan(h_lru, l, p, d):
    S = h_lru.shape[0]
    tm = min(512, S)
    nblk = S // tm
    W = LRU_W
    rev = d == 1
    hb = tm // SUBLANES
    nrow8 = S // SUBLANES
    blk = (lambda c: nblk - 1 - c) if rev else (lambda c: c)
    const = lambda shape: pl.BlockSpec(shape, lambda c: (0,) * len(shape))
    return pl.pallas_call(
        functools.partial(_lru_kernel, rev, tm, nblk),
        grid=(nblk,),
        in_specs=[
            pl.BlockSpec((tm, W), lambda c: (blk(c), 0)),
            pl.BlockSpec((SUBLANES, W), lambda c: (jnp.maximum(blk(c) * hb - 1, 0), 0)),
            pl.BlockSpec((SUBLANES, W), lambda c: (jnp.minimum((blk(c) + 1) * hb, nrow8 - 1), 0)),
            const((CONV_W, W)), const((1, W)),
            const((LRU_BLOCKS, LRU_BLOCK, LRU_BLOCK)), const((1, W)),
            const((LRU_BLOCKS, LRU_BLOCK, LRU_BLOCK)), const((1, W)), const((1, W)),
        ],
        out_specs=pl.BlockSpec((tm, W), lambda c: (blk(c), 0)),
        out_shape=jax.ShapeDtypeStruct((S, W), F32),
        scratch_shapes=[pltpu.VMEM((SUBLANES, W), F32)],
        compiler_params=_cparams(("arbitrary",)),
        name="lru_scan_rev" if rev else "lru_scan_fwd",
    )(h_lru, h_lru, h_lru, p['lru_conv_w'][l], p['lru_conv_b'][l].reshape(1, W),
      p['lru_wa'][l, d].astype(BF16), p['lru_ba'][l, d].reshape(1, W),
      p['lru_wx'][l, d].astype(BF16), p['lru_bx'][l, d].reshape(1, W), p['lru_lambda'][l, d].reshape(1, W))


MLA_QK = 2 * LANES
MLA_Q_SCALE = float((QK_NOPE + QK_ROPE) ** -0.5 * np.log2(np.e))


def _mla_prep_kernel(cq_ref, ckv_ref, kr_ref, pos_ref, invf_ref, qn_ref, kvn_ref, wq_ref, wk_ref, wv_ref,
                     q_o, k_o, v_o):
    def norm(x, g):
        ms = jnp.mean(x * x, axis=-1, keepdims=True)
        return (x * lax.rsqrt(ms + NORM_EPS) * g).astype(BF16)

    q = _dot(norm(cq_ref[...], qn_ref[...]), wq_ref[...]) * MLA_Q_SCALE
    ckv = norm(ckv_ref[...], kvn_ref[...])
    kn = _dot(ckv, wk_ref[...])
    v = _dot(ckv, wv_ref[...])
    ang = pos_ref[...].astype(F32) * invf_ref[...]
    lane = lax.broadcasted_iota(jnp.int32, (1, LANES), 1)
    half = QK_ROPE // 2
    cos = jnp.cos(ang)
    sin = jnp.sin(ang)
    s_lo = jnp.where(lane < half, -sin, 0.0)
    s_hi = jnp.where((lane >= half) & (lane < QK_ROPE), sin, 0.0)

    def rope(t):
        return t * cos + pltpu.roll(t, LANES - half, 1) * s_lo + pltpu.roll(t, half, 1) * s_hi

    kpe = rope(kr_ref[...]).astype(BF16)
    for hh in range(MLA_HEADS):
        q_o[hh, :, 0:LANES] = q[:, hh * MLA_QK:hh * MLA_QK + LANES].astype(BF16)
        q_o[hh, :, LANES:] = rope(q[:, hh * MLA_QK + LANES:(hh + 1) * MLA_QK]).astype(BF16)
        k_o[hh, :, 0:LANES] = kn[:, hh * QK_NOPE:(hh + 1) * QK_NOPE].astype(BF16)
        k_o[hh, :, LANES:] = kpe
        v_o[hh, :, 0:V_HEAD] = v[:, hh * V_HEAD:(hh + 1) * V_HEAD].astype(BF16)
        v_o[hh, :, V_HEAD:] = jnp.ones((v.shape[0], V_HEAD), BF16)


def _mla_prep(h_mla, positions, l, p):
    S = h_mla.shape[0]
    tm = min(512, S)
    half = QK_ROPE // 2
    inv = ROPE_THETA ** (-jnp.arange(0, QK_ROPE, 2, dtype=F32) / QK_ROPE)
    invf = jnp.zeros((1, LANES), F32).at[0, :half].set(inv).at[0, half:QK_ROPE].set(inv)
    wq = p['mla_w_uq'][l].reshape(Q_LORA, MLA_HEADS, QK_NOPE + QK_ROPE)
    wq = jnp.concatenate([wq, jnp.zeros((Q_LORA, MLA_HEADS, MLA_QK - QK_NOPE - QK_ROPE), F32)], axis=-1)
    wq = wq.reshape(Q_LORA, MLA_HEADS * MLA_QK).astype(BF16)
    const = lambda shape: pl.BlockSpec(shape, lambda i: (0,) * len(shape))
    hq = jax.ShapeDtypeStruct((MLA_HEADS, S, MLA_QK), BF16)
    return pl.pallas_call(
        _mla_prep_kernel,
        grid=(S // tm,),
        in_specs=[
            pl.BlockSpec((tm, Q_LORA), lambda i: (i, 0)),
            pl.BlockSpec((tm, KV_LORA), lambda i: (i, Q_LORA // KV_LORA)),
            pl.BlockSpec((tm, LANES), lambda i: (i, MLA_COL_KR // LANES)),
            pl.BlockSpec((tm, 1), lambda i: (i, 0)),
            const((1, LANES)), const((1, Q_LORA)), const((1, KV_LORA)),
            const((Q_LORA, MLA_HEADS * MLA_QK)), const((KV_LORA, MLA_HEADS * QK_NOPE)),
            const((KV_LORA, MLA_HEADS * V_HEAD)),
        ],
        out_specs=[pl.BlockSpec((MLA_HEADS, tm, MLA_QK), lambda i: (0, i, 0)),
                   pl.BlockSpec((MLA_HEADS, tm, MLA_QK), lambda i: (0, i, 0)),
                   pl.BlockSpec((MLA_HEADS, tm, 2 * V_HEAD), lambda i: (0, i, 0))],
        out_shape=[hq, hq, jax.ShapeDtypeStruct((MLA_HEADS, S, 2 * V_HEAD), BF16)],
        compiler_params=_cparams(("parallel",)),
        name="mla_prep",
    )(h_mla, h_mla, h_mla, positions.reshape(S, 1), invf, p['mla_q_norm'][l].reshape(1, Q_LORA),
      p['mla_kv_norm'][l].reshape(1, KV_LORA), wq, p['mla_w_uk'][l].astype(BF16), p['mla_w_uv'][l].astype(BF16))


FLASH_ROWS = 2


def _flash_kernel(nkv, tq, q_ref, k_ref, v_ref, o_ref, m_ref, acc_ref):
    j = pl.program_id(2)

    @pl.when(j == 0)
    def _():
        m_ref[...] = jnp.full_like(m_ref, -jnp.inf)
        acc_ref[...] = jnp.zeros_like(acc_ref)

    tr = tq // FLASH_ROWS
    k = k_ref[...]
    v = v_ref[...]
    scores = [_dot_nt(q_ref[g * tr:(g + 1) * tr, :], k) for g in range(FLASH_ROWS)]
    for g in range(FLASH_ROWS):
        rows = slice(g * tr, (g + 1) * tr)
        s = scores[g]
        m_old = m_ref[rows, :]
        m_new = jnp.maximum(m_old, jnp.max(s, axis=-1, keepdims=True))
        pm = jnp.exp2(s - m_new).astype(BF16)
        acc_ref[rows, :] = jnp.exp2(m_old - m_new) * acc_ref[rows, :] + _dot(pm, v)
        m_ref[rows, :] = m_new

    @pl.when(j == nkv - 1)
    def _():
        o_ref[...] = acc_ref[:, 0:V_HEAD] / acc_ref[:, V_HEAD:]


def _flash_attention(q, k, v):
    S = q.shape[1]
    tq = min(1024, S)
    tk = min(2048, S)
    nkv = S // tk
    return pl.pallas_call(
        functools.partial(_flash_kernel, nkv, tq),
        grid=(MLA_HEADS, S // tq, nkv),
        in_specs=[pl.BlockSpec((None, tq, MLA_QK), lambda h, i, j: (h, i, 0)),
                  pl.BlockSpec((None, tk, MLA_QK), lambda h, i, j: (h, j, 0)),
                  pl.BlockSpec((None, tk, 2 * V_HEAD), lambda h, i, j: (h, j, 0))],
        out_specs=pl.BlockSpec((tq, V_HEAD), lambda h, i, j: (i, h)),
        out_shape=jax.ShapeDtypeStruct((S, MLA_W), F32),
        scratch_shapes=[pltpu.VMEM((tq, 1), F32), pltpu.VMEM((tq, 2 * V_HEAD), F32)],
        compiler_params=_cparams(("parallel", "parallel", "arbitrary")),
        name="mla_flash",
    )(q, k, v)


def _merge_kernel(ya_ref, h0_ref, h1_ref, gb_ref, yc_ref, g0_ref, g1_ref, g2_ref, wa_ref, wb_ref, wc_ref, o_ref):
    yb = (h0_ref[...] + h1_ref[...]) * jax.nn.gelu(gb_ref[...])
    m = _sigmoid(g0_ref[...]) * _dot(ya_ref[...].astype(BF16), wa_ref[...])
    m = m + _sigmoid(g1_ref[...]) * _dot(yb.astype(BF16), wb_ref[...])
    m = m + _sigmoid(g2_ref[...]) * _dot(yc_ref[...].astype(BF16), wc_ref[...])
    o_ref[...] = m.astype(o_ref.dtype)


def _merge(ya, hs0, hs1, h_lru, yc, gates, l, p):
    S = ya.shape[0]
    tm = min(512, S)
    tn = 512
    nj = D_MODEL // tn
    rowa = pl.BlockSpec((tm, RWKV_W), lambda j, i: (i, 0))
    gate = lambda b: pl.BlockSpec((tm, tn), lambda j, i: (i, b * nj + j))
    wcol = lambda k: pl.BlockSpec((k, tn), lambda j, i: (0, j))
    return pl.pallas_call(
        _merge_kernel,
        grid=(nj, S // tm),
        in_specs=[rowa, rowa, rowa, pl.BlockSpec((tm, LRU_W), lambda j, i: (i, 1)),
                  pl.BlockSpec((tm, MLA_W), lambda j, i: (i, 0)),
                  gate(0), gate(1), gate(2), wcol(RWKV_W), wcol(LRU_W), wcol(MLA_W)],
        out_specs=pl.BlockSpec((tm, tn), lambda j, i: (i, j)),
        out_shape=jax.ShapeDtypeStruct((S, D_MODEL), BF16),
        compiler_params=_cparams(("parallel", "parallel")),
        name="merge",
    )(ya, hs0, hs1, h_lru, yc, gates, gates, gates,
      p['wo_rwkv'][l].astype(BF16), p['wo_lru'][l].astype(BF16), p['wo_mla'][l].astype(BF16))


def _router_kernel(x_ref, g_ref, wr_ref, xn_o, lg_o):
    x = x_ref[...]
    ms = jnp.mean(x * x, axis=-1, keepdims=True)
    xn = x * lax.rsqrt(ms + NORM_EPS) * g_ref[...]
    xn_o[...] = xn
    lg_o[...] = _dot3_nt(wr_ref[...], xn)


def _router(x, gain, w_router):
    S = x.shape[0]
    tm = min(512, S)
    return pl.pallas_call(
        _router_kernel,
        grid=(S // tm,),
        in_specs=[pl.BlockSpec((tm, D_MODEL), lambda i: (i, 0)), pl.BlockSpec((1, D_MODEL), lambda i: (0, 0)),
                  pl.BlockSpec((N_EXPERTS, D_MODEL), lambda i: (0, 0))],
        out_specs=[pl.BlockSpec((tm, D_MODEL), lambda i: (i, 0)), pl.BlockSpec((N_EXPERTS, tm), lambda i: (0, i))],
        out_shape=[jax.ShapeDtypeStruct((S, D_MODEL), F32), jax.ShapeDtypeStruct((N_EXPERTS, S), F32)],
        compiler_params=_cparams(("parallel",)),
        name="moe_router",
    )(x, gain.reshape(1, D_MODEL), w_router.T)


def _route_kernel(nb, cap, lg_ref, idx_o, gate_o, aff_ref):
    E = N_EXPERTS
    m = lg_ref[0]
    for e in range(1, E):
        m = jnp.maximum(m, lg_ref[e])
    z = jnp.zeros_like(m)
    for e in range(E):
        z = z + jnp.exp(lg_ref[e] - m)
    for e in range(E):
        aff_ref[e] = jnp.exp(lg_ref[e] - m) / z

    li = lax.broadcasted_iota(jnp.int32, (LANES, LANES), 0)
    lj = lax.broadcasted_iota(jnp.int32, (LANES, LANES), 1)
    upper = jnp.where(li <= lj, 1.0, 0.0).astype(BF16)
    bi = lax.broadcasted_iota(jnp.int32, (nb, nb), 0)
    bj = lax.broadcasted_iota(jnp.int32, (nb, nb), 1)
    lstrict = jnp.where(bj < bi, 1.0, 0.0).astype(BF16)
    qrow = lax.broadcasted_iota(jnp.int32, (1, cap), 1).astype(F32)
    brow = lax.broadcasted_iota(jnp.int32, (nb, cap), 0).astype(F32)
    irow = lax.broadcasted_iota(jnp.int32, (LANES, cap), 0).astype(F32)

    def prefix(mask):
        cum = _dot(mask.astype(BF16), upper)
        tot = cum[:, LANES - 1:LANES]
        off = _dot(lstrict, jnp.broadcast_to(tot, (nb, LANES)).astype(BF16))
        return cum, off, tot

    def expert(e, carry):
        aff = aff_ref[e]
        bits = pltpu.bitcast(aff, jnp.int32)

        def bisect(_, lohi):
            lo, hi = lohi
            mid = lo + lax.shift_right_arithmetic(hi - lo, 1)
            cnt = jnp.sum(jnp.where(bits >= mid, 1.0, 0.0), keepdims=True)
            ok = cnt >= cap
            return jnp.where(ok, mid, lo), jnp.where(ok, hi, mid)

        lo0 = jnp.zeros((1, 1), jnp.int32)
        hi0 = jnp.full((1, 1), 0x7F800000, jnp.int32)
        thr, _ = lax.fori_loop(0, 31, bisect, (lo0, hi0))
        gt = jnp.where(bits > thr, 1.0, 0.0)
        eq = jnp.where(bits == thr, 1.0, 0.0)
        need = cap - jnp.sum(gt, keepdims=True)
        ecum, eoff, _ = prefix(eq)
        sel = gt + eq * jnp.where(ecum + eoff <= need, 1.0, 0.0)
        cum, off, tot = prefix(sel)
        full = jnp.where(off[:, 0:1] + tot <= qrow, 1.0, 0.0)
        nfull = jnp.sum(full, axis=0, keepdims=True)
        offsel = jnp.sum(full * tot, axis=0, keepdims=True)
        onehot_b = jnp.where(brow == nfull, 1.0, 0.0).astype(BF16)
        cum_t = _dot_tn(cum.astype(BF16), onehot_b)
        within = jnp.sum(jnp.where(cum_t <= qrow - offsel, 1.0, 0.0), axis=0, keepdims=True)
        idx_o[e] = (nfull * LANES + within).astype(jnp.int32)
        a1 = aff.astype(BF16)
        r1 = aff - a1.astype(F32)
        a2 = r1.astype(BF16)
        a3 = (r1 - a2.astype(F32)).astype(BF16)
        aff_t = _dot_tn(a1, onehot_b) + _dot_tn(a2, onehot_b) + _dot_tn(a3, onehot_b)
        gate_o[e] = jnp.sum(jnp.where(irow == within, aff_t, 0.0), axis=0, keepdims=True)
        return carry

    lax.fori_loop(0, E, expert, 0)


def _route(logits_t, cap):
    E, S = logits_t.shape
    nb = S // LANES
    return pl.pallas_call(
        functools.partial(_route_kernel, nb, cap),
        out_shape=[jax.ShapeDtypeStruct((E, 1, cap), jnp.int32), jax.ShapeDtypeStruct((E, 1, cap), F32)],
        scratch_shapes=[pltpu.VMEM((E, nb, LANES), F32)],
        compiler_params=pltpu.CompilerParams(vmem_limit_bytes=VMEM_LIMIT),
        name="moe_route",
    )(logits_t.reshape(E, nb, LANES))


DMA_UNROLL = 8


def _expert_kernel(tq, nt, idx_ref, xacc_hbm, xn_hbm, gate_ref, wg_ref, wu_ref, wd_ref, out_hbm, xs_ref, ob_ref, sems):
    del xacc_hbm
    e = pl.program_id(0)
    t = pl.program_id(1)
    base = (e * nt + t) * tq

    def token_copy(r):
        tok = idx_ref[base + r]
        return pltpu.make_async_copy(xn_hbm.at[pl.ds(tok, 1), :], xs_ref.at[pl.ds(r, 1), :], sems.at[0])

    def row_copy(r):
        tok = idx_ref[base + r]
        return pltpu.make_async_copy(out_hbm.at[pl.ds(tok, 1), :], ob_ref.at[pl.ds(r, 1), :], sems.at[1])

    def scatter_copy(r):
        tok = idx_ref[base + r]
        return pltpu.make_async_copy(ob_ref.at[pl.ds(r, 1), :], out_hbm.at[pl.ds(tok, 1), :], sems.at[2])

    def start_gather(r, c):
        token_copy(r).start()
        row_copy(r).start()
        return c

    def wait_tokens(r, c):
        token_copy(r).wait()
        return c

    def wait_rows(r, c):
        row_copy(r).wait()
        return c

    lax.fori_loop(0, tq, start_gather, 0, unroll=DMA_UNROLL)
    lax.fori_loop(0, tq, wait_tokens, 0, unroll=DMA_UNROLL)
    xs = xs_ref[...].astype(BF16)
    hg = _dot(xs, wg_ref[...])
    hu = _dot(xs, wu_ref[...])
    hdn = (hg * _sigmoid(hg)) * hu
    ye = _dot(hdn.astype(BF16), wd_ref[...])
    lax.fori_loop(0, tq, wait_rows, 0, unroll=DMA_UNROLL)
    ob_ref[...] = ob_ref[...] + ye * gate_ref[...]

    def start_scatter(r, c):
        scatter_copy(r).start()
        return c

    def wait_scatter(r, c):
        scatter_copy(r).wait()
        return c

    lax.fori_loop(0, tq, start_scatter, 0, unroll=DMA_UNROLL)
    lax.fori_loop(0, tq, wait_scatter, 0, unroll=DMA_UNROLL)


def _experts(x, xn, idx, gate, l, p):
    S = x.shape[0]
    E, cap = idx.shape
    tq = min(256, cap)
    nt = cap // tq
    grid_spec = pltpu.PrefetchScalarGridSpec(
        num_scalar_prefetch=1,
        grid=(E, nt),
        in_specs=[
            pl.BlockSpec(memory_space=pl.ANY),
            pl.BlockSpec(memory_space=pl.ANY),
            pl.BlockSpec((None, tq, 1), lambda e, t, idx: (e, t, 0)),
            pl.BlockSpec((None, D_MODEL, EXPERT_FF), lambda e, t, idx: (e, 0, 0)),
            pl.BlockSpec((None, D_MODEL, EXPERT_FF), lambda e, t, idx: (e, 0, 0)),
            pl.BlockSpec((None, EXPERT_FF, D_MODEL), lambda e, t, idx: (e, 0, 0)),
        ],
        out_specs=pl.BlockSpec(memory_space=pl.ANY),
        scratch_shapes=[pltpu.VMEM((tq, D_MODEL), F32), pltpu.VMEM((tq, D_MODEL), F32),
                        pltpu.SemaphoreType.DMA((3,))],
    )
    return pl.pallas_call(
        functools.partial(_expert_kernel, tq, nt),
        grid_spec=grid_spec,
        out_shape=jax.ShapeDtypeStruct((S, D_MODEL), F32),
        input_output_aliases={1: 0},
        compiler_params=_cparams(("arbitrary", "arbitrary")),
        name="moe_experts",
    )(idx.reshape(E * cap), x, xn, gate.reshape(E, cap, 1),
      p['w_gate'][l].astype(BF16), p['w_up'][l].astype(BF16), p['w_down'][l].astype(BF16))


def _trunk(x, positions, p):
    S = x.shape[0]
    cap = CAPACITY_FACTOR * S // N_EXPERTS
    depth = p['w_in'].shape[0]
    v_first = None
    for l in range(depth):
        xn = _rmsnorm(x, p['norm_mix'][l], BF16)
        h_rwkv, h_lru, h_mla, gates = _in_projection(xn, p['w_in'][l], p['rwkv_vres_down'][l - 1] if l > 0 else None)
        r, v, kk, lw, kd, a, g, bonus = _rwkv_prep(h_rwkv, h_mla, l, p, v_first)
        if l == 0:
            v_first = v
        yf, yb = _rwkv_scan(r, v, kk, lw, kd, a)
        ya = _rwkv_post(yf, yb, bonus, g, p['rwkv_lnx_w'][l], p['rwkv_lnx_b'][l])
        hs0 = _lru_scan(h_lru, l, p, 0)
        hs1 = _lru_scan(h_lru, l, p, 1)
        q, k, vv = _mla_prep(h_mla, positions, l, p)
        yc = _flash_attention(q, k, vv)
        merged = _merge(ya, hs0, hs1, h_lru, yc, gates, l, p)
        x = _matmul(merged, p['w_out'][l].astype(BF16), D_MODEL, res=x, name="outproj")
        xn2, logits_t = _router(x, p['norm_ffn'][l], p['w_router'][l])
        idx, gate = _route(logits_t, cap)
        x = _experts(x, xn2, idx.reshape(N_EXPERTS, cap), gate.reshape(N_EXPERTS, cap), l, p)
    return _rmsnorm(x, p['norm_final'], F32)


def kernel(x, positions, norm_mix, w_in, rwkv_shift_mu, rwkv_w0, rwkv_w2, rwkv_a0, rwkv_a2, rwkv_g2, rwkv_k_k,
           rwkv_k_a, rwkv_r_k, rwkv_lnx_w, rwkv_lnx_b, rwkv_vres_down, rwkv_vres_up, rwkv_vres_b, lru_conv_w,
           lru_conv_b, lru_wa, lru_ba, lru_wx, lru_bx, lru_lambda, mla_q_norm, mla_kv_norm, mla_w_uq, mla_w_uk,
           mla_w_uv, wo_rwkv, wo_lru, wo_mla, w_out, norm_ffn, w_router, w_gate, w_up, w_down, norm_final):
    p = dict(norm_mix=norm_mix, w_in=w_in, rwkv_shift_mu=rwkv_shift_mu, rwkv_w0=rwkv_w0, rwkv_w2=rwkv_w2,
             rwkv_a0=rwkv_a0, rwkv_a2=rwkv_a2, rwkv_g2=rwkv_g2, rwkv_k_k=rwkv_k_k, rwkv_k_a=rwkv_k_a,
             rwkv_r_k=rwkv_r_k, rwkv_lnx_w=rwkv_lnx_w, rwkv_lnx_b=rwkv_lnx_b, rwkv_vres_down=rwkv_vres_down,
             rwkv_vres_up=rwkv_vres_up, rwkv_vres_b=rwkv_vres_b, lru_conv_w=lru_conv_w, lru_conv_b=lru_conv_b,
             lru_wa=lru_wa, lru_ba=lru_ba, lru_wx=lru_wx, lru_bx=lru_bx, lru_lambda=lru_lambda,
             mla_q_norm=mla_q_norm, mla_kv_norm=mla_kv_norm, mla_w_uq=mla_w_uq, mla_w_uk=mla_w_uk,
             mla_w_uv=mla_w_uv, wo_rwkv=wo_rwkv, wo_lru=wo_lru, wo_mla=wo_mla, w_out=w_out, norm_ffn=norm_ffn,
             w_router=w_router, w_gate=w_gate, w_up=w_up, w_down=w_down, norm_final=norm_final)
    B, S, D = x.shape
    outs = [_trunk(x[b], positions[b], p) for b in range(B)]
    return jnp.stack(outs, axis=0)
```

```python
import functools

import numpy as np
import jax
import jax.numpy as jnp
from jax import lax
from jax.experimental import pallas as pl
from jax.experimental.pallas import tpu as pltpu

F32 = jnp.float32
BF16 = jnp.bfloat16

D_MODEL = 2048
RWKV_HEADS = 12
RWKV_HEAD = 64
RWKV_W = RWKV_HEADS * RWKV_HEAD
DECAY_LORA = 64
ICLR_LORA = 64
VRES_LORA = 32
GATE_LORA = 128
GN_EPS = 64e-5
LRU_BLOCKS = 6
LRU_BLOCK = 128
LRU_W = LRU_BLOCKS * LRU_BLOCK
CONV_W = 4
LRU_C = 8.0
MLA_HEADS = 4
QK_NOPE = 128
QK_ROPE = 64
V_HEAD = 128
Q_LORA = 384
KV_LORA = 128
MLA_W = MLA_HEADS * V_HEAD
ROPE_THETA = 10000.0
N_BRANCH = 3
N_EXPERTS = 16
EXPERT_FF = 1024
CAPACITY_FACTOR = 2
NORM_EPS = 1e-6
RWKV_COLS = 3 * RWKV_W + 2 * DECAY_LORA + 2 * ICLR_LORA + GATE_LORA

LANES = 128
SUBLANES = 8
VMEM_LIMIT = 56 * 1024 * 1024

SEG_LRU = RWKV_COLS
SEG_MLA = SEG_LRU + 2 * LRU_W
SEG_GATES = SEG_MLA + Q_LORA + KV_LORA + QK_ROPE
MLA_SEG_W = Q_LORA + KV_LORA + 2 * LANES
MLA_COL_KR = Q_LORA + KV_LORA
MLA_COL_VRES = MLA_COL_KR + LANES

CHUNK = 64
PAIR = 2 * RWKV_HEAD


def _cparams(sem):
    return pltpu.CompilerParams(dimension_semantics=sem, vmem_limit_bytes=VMEM_LIMIT)


def _dot(a, b):
    return jnp.dot(a, b, preferred_element_type=F32)


def _dot_nt(a, b):
    return lax.dot_general(a, b, (((1,), (1,)), ((), ())), preferred_element_type=F32)


def _dot_tn(a, b):
    return lax.dot_general(a, b, (((0,), (0,)), ((), ())), preferred_element_type=F32)


def _split(x):
    hi = x.astype(BF16)
    lo = (x - hi.astype(F32)).astype(BF16)
    return hi, lo


def _dot_exactb(x, b):
    hi, lo = _split(x)
    return _dot(hi, b) + _dot(lo, b)


def _dot3(a, b):
    ah, al = _split(a)
    bh, bl = _split(b)
    return _dot(ah, bh) + (_dot(ah, bl) + _dot(al, bh))


def _dot3_nt(a, b):
    ah, al = _split(a)
    bh, bl = _split(b)
    return _dot_nt(ah, bh) + (_dot_nt(ah, bl) + _dot_nt(al, bh))


def _dot3_tn(a, b):
    ah, al = _split(a)
    bh, bl = _split(b)
    return _dot_tn(ah, bh) + (_dot_tn(ah, bl) + _dot_tn(al, bh))


def _sigmoid(x):
    return 1.0 / (1.0 + jnp.exp(-x))


def _softplus(x):
    return jnp.maximum(x, 0.0) + jnp.log(1.0 + jnp.exp(-jnp.abs(x)))


def _rmsnorm_kernel(x_ref, g_ref, o_ref):
    x = x_ref[...]
    ms = jnp.mean(x * x, axis=-1, keepdims=True)
    o_ref[...] = (x * lax.rsqrt(ms + NORM_EPS) * g_ref[...]).astype(o_ref.dtype)


def _rmsnorm(x, gain, dtype):
    S, D = x.shape
    tm = min(512, S)
    return pl.pallas_call(
        _rmsnorm_kernel,
        grid=(S // tm,),
        in_specs=[pl.BlockSpec((tm, D), lambda i: (i, 0)), pl.BlockSpec((1, D), lambda i: (0, 0))],
        out_specs=pl.BlockSpec((tm, D), lambda i: (i, 0)),
        out_shape=jax.ShapeDtypeStruct((S, D), dtype),
        compiler_params=_cparams(("parallel",)),
        name="rmsnorm",
    )(x, gain.reshape(1, D))


def _matmul_kernel(has_res, *refs):
    if has_res:
        a_ref, w_ref, res_ref, o_ref = refs
        o_ref[...] = res_ref[...] + _dot(a_ref[...], w_ref[...])
    else:
        a_ref, w_ref, o_ref = refs
        o_ref[...] = _dot(a_ref[...], w_ref[...])


def _matmul(a, w, tn, res=None, name="matmul"):
    M, K = a.shape
    N = w.shape[1]
    tm = min(512, M)
    in_specs = [pl.BlockSpec((tm, K), lambda j, i: (i, 0)), pl.BlockSpec((K, tn), lambda j, i: (0, j))]
    args = [a, w]
    if res is not None:
        in_specs.append(pl.BlockSpec((tm, tn), lambda j, i: (i, j)))
        args.append(res)
    return pl.pallas_call(
        functools.partial(_matmul_kernel, res is not None),
        grid=(N // tn, M // tm),
        in_specs=in_specs,
        out_specs=pl.BlockSpec((tm, tn), lambda j, i: (i, j)),
        out_shape=jax.ShapeDtypeStruct((M, N), F32),
        compiler_params=_cparams(("parallel", "parallel")),
        name=name,
    )(*args)


def _in_projection(xn, w_in, vres_down):
    w = w_in.astype(BF16)
    w_mla = jnp.zeros((D_MODEL, MLA_SEG_W), BF16)
    w_mla = w_mla.at[:, :MLA_COL_KR + QK_ROPE].set(w[:, SEG_MLA:SEG_GATES])
    if vres_down is not None:
        w_mla = w_mla.at[:, MLA_COL_VRES:MLA_COL_VRES + VRES_LORA].set(vres_down.astype(BF16))
    h_rwkv = _matmul(xn, w[:, :SEG_LRU], RWKV_COLS, name="inproj_rwkv")
    h_lru = _matmul(xn, w[:, SEG_LRU:SEG_MLA], 2 * LRU_W, name="inproj_lru")
    h_mla = _matmul(xn, w_mla, MLA_SEG_W, name="inproj_mla")
    return h_rwkv, h_lru, h_mla, w[:, SEG_GATES:]


def _rwkv_prep_kernel(has_vres, tm, nblk, *refs):
    if has_vres:
        (h_ref, hp_ref, hn_ref, lora_ref, vfirst_ref, vup_ref, vb_ref, *rest) = refs
    else:
        (h_ref, hp_ref, hn_ref, *rest) = refs
    (mu_ref, w0_ref, w2_ref, a0_ref, a2_ref, g2_ref, kkw_ref, ka_ref, rk_ref, bd_ref,
     r_o, v_o, kk_o, lw_o, kd_o, a_o, g_o, bonus_o) = rest
    i = pl.program_id(0)
    p = h_ref[...]
    rows = lax.broadcasted_iota(jnp.int32, (tm, 1), 0)
    prev_row = jnp.where(i == 0, 0.0, hp_ref[SUBLANES - 1:SUBLANES, :])
    next_row = jnp.where(i == nblk - 1, 0.0, hn_ref[0:1, :])
    prev = jnp.where(rows == 0, prev_row, pltpu.roll(p, 1, 0))
    nxt = jnp.where(rows == tm - 1, next_row, pltpu.roll(p, tm - 1, 0))
    hs = p + mu_ref[0:1, :] * (prev - p) + mu_ref[1:2, :] * (nxt - p)

    W = RWKV_W
    r = hs[:, 0:W]
    k = hs[:, W:2 * W]
    v = hs[:, 2 * W:3 * W]
    wd = hs[:, 3 * W:3 * W + LANES]
    ad = hs[:, 3 * W + LANES:3 * W + 2 * LANES]
    gd = hs[:, 3 * W + 2 * LANES:3 * W + 3 * LANES]
    if has_vres:
        vg = _sigmoid(vb_ref[...] + _dot(lora_ref[...].astype(BF16), vup_ref[...]))
        v = v + (vfirst_ref[...] - v) * vg
    bd = bd_ref[...]
    kk = k * kkw_ref[...]
    n2 = _dot_exactb(kk * kk, bd)
    kk = kk / jnp.maximum(jnp.sqrt(n2), 1e-12)
    tw = jnp.tanh(wd).astype(BF16)
    adb = ad.astype(BF16)
    ksum = None
    for d in range(2):
        wl = w0_ref[d:d + 1, :] + _dot(tw, w2_ref[d])
        w_log = -_softplus(-wl) - 0.5
        lw_o[d] = -jnp.exp(w_log)
        a = _sigmoid(a0_ref[d:d + 1, :] + _dot(adb, a2_ref[d]))
        kd = k * (1.0 + (a - 1.0) * ka_ref[...])
        a_o[d] = a
        kd_o[d] = kd
        ksum = kd if ksum is None else ksum + kd
    r_o[...] = r
    v_o[...] = v
    kk_o[...] = kk
    g_o[...] = _dot(_sigmoid(gd).astype(BF16), g2_ref[...])
    bonus_o[...] = _dot_exactb(r * ksum * rk_ref[...], bd) * v


def _head_blockdiag():
    hid = np.arange(RWKV_W) // RWKV_HEAD
    return jnp.asarray((hid[:, None] == hid[None, :]).astype(np.float32), dtype=BF16)


def _rwkv_prep(h, h_mla, l, p, v_first):
    S = h.shape[0]
    tm = min(256, S)
    nblk = S // tm
    has_vres = l > 0
    W = RWKV_W
    hb = tm // SUBLANES
    nrow8 = S // SUBLANES

    def zpad(w, rows_before):
        z = jnp.zeros((LANES, W), F32)
        return z.at[rows_before:rows_before + w.shape[0]].set(w)

    w2 = jnp.stack([zpad(p['rwkv_w2'][l, 0], 0), zpad(p['rwkv_w2'][l, 1], DECAY_LORA)]).astype(BF16)
    a2 = jnp.stack([zpad(p['rwkv_a2'][l, 0], 0), zpad(p['rwkv_a2'][l, 1], ICLR_LORA)]).astype(BF16)
    const2 = lambda shape: pl.BlockSpec(shape, lambda i: (0,) * len(shape))
    in_specs = [
        pl.BlockSpec((tm, RWKV_COLS), lambda i: (i, 0)),
        pl.BlockSpec((SUBLANES, RWKV_COLS), lambda i: (jnp.maximum(i * hb - 1, 0), 0)),
        pl.BlockSpec((SUBLANES, RWKV_COLS), lambda i: (jnp.minimum((i + 1) * hb, nrow8 - 1), 0)),
    ]
    args = [h, h, h]
    if has_vres:
        in_specs += [
            pl.BlockSpec((tm, LANES), lambda i: (i, MLA_COL_VRES // LANES)),
            pl.BlockSpec((tm, W), lambda i: (i, 0)),
            const2((LANES, W)),
            const2((1, W)),
        ]
        vup = jnp.zeros((LANES, W), F32).at[:VRES_LORA].set(p['rwkv_vres_up'][l - 1]).astype(BF16)
        args += [h_mla, v_first, vup, p['rwkv_vres_b'][l - 1].reshape(1, W)]
    in_specs += [
        const2((2, RWKV_COLS)), const2((2, W)), const2((2, LANES, W)), const2((2, W)), const2((2, LANES, W)),
        const2((LANES, W)), const2((1, W)), const2((1, W)), const2((1, W)), const2((W, W)),
    ]
    args += [
        p['rwkv_shift_mu'][l], p['rwkv_w0'][l], w2, p['rwkv_a0'][l], a2,
        p['rwkv_g2'][l].astype(BF16), p['rwkv_k_k'][l].reshape(1, W), p['rwkv_k_a'][l].reshape(1, W),
        p['rwkv_r_k'][l].reshape(1, W), _head_blockdiag(),
    ]
    row = pl.BlockSpec((tm, W), lambda i: (i, 0))
    row2 = pl.BlockSpec((2, tm, W), lambda i: (0, i, 0))
    sds = jax.ShapeDtypeStruct((S, W), F32)
    sds2 = jax.ShapeDtypeStruct((2, S, W), F32)
    return pl.pallas_call(
        functools.partial(_rwkv_prep_kernel, has_vres, tm, nblk),
        grid=(nblk,),
        in_specs=in_specs,
        out_specs=[row, row, row, row2, row2, row2, row, row],
        out_shape=[sds, sds, sds, sds2, sds2, sds2, sds, sds],
        compiler_params=_cparams(("parallel",)),
        name="rwkv_prep",
    )(*args)


N_MASK = 8


def _scan_masks():
    n = PAIR
    out = np.zeros((2, N_MASK, n, n), np.float32)
    r = np.arange(n)[:, None]
    c = np.arange(n)[None, :]
    same = (r // CHUNK) == (c // CHUNK)
    for d in range(2):
        before = (c < r) if d == 0 else (c > r)
        strict = same & before
        out[d, 0] = strict
        out[d, 1] = same & (before | (c == r))
        out[d, 2] = strict & ((r // 8) == (c // 8))
        for j, b in enumerate((8, 16, 32)):
            out[d, 3 + j] = strict & ((r // (2 * b)) == (c // (2 * b))) & ((r // b) != (c // b))
        out[d, 6] = (r == c)
        out[d, 7] = (before | (c == r)) & (r < CHUNK) & (c < CHUNK)
    return jnp.asarray(out)


def _mm(a, b):
    return _dot(a.astype(BF16), b.astype(BF16))


def _mm_nt(a, b):
    return _dot_nt(a.astype(BF16), b.astype(BF16))


def _mm_tn(a, b):
    return _dot_tn(a.astype(BF16), b.astype(BF16))


def _rwkv_scan_kernel(nsub, rf, vf, kkf, lwf, kf, af, rb, vb, kkb, lwb, kb_, ab, m_ref, yf, yb, s_ref):
    T = CHUNK
    dirs = ((rf, vf, kkf, lwf, kf, af, yf), (rb, vb, kkb, lwb, kb_, ab, yb))

    @pl.when(pl.program_id(1) == 0)
    def _():
        s_ref[...] = jnp.zeros_like(s_ref)

    lane = lax.broadcasted_iota(jnp.int32, (1, PAIR), 1)
    m0 = jnp.where(lane < RWKV_HEAD, 1.0, 0.0)
    m1 = 1.0 - m0

    def stack(x):
        return jnp.concatenate([x * m0, x * m1], axis=0)

    def local(d, j):
        r_ref, v_ref, kk_ref, lw_ref, k_ref, a_ref, _ = dirs[d]
        sl = slice(j * T, (j + 1) * T)
        strict, incl, eye = m_ref[d, 0], m_ref[d, 1], m_ref[d, 6]
        lw = lw_ref[sl, :]
        kk = kk_ref[sl, :]
        beta = a_ref[sl, :] * kk
        kd = k_ref[sl, :]
        cs = m_ref[d, 7][0:T, 0:T].astype(BF16)
        lh, ll = _split(lw)
        c = _dot(cs, lh) + _dot(cs, ll)
        c_tot = c[0:1, :] if d == 1 else c[T - 1:T, :]
        e_inv = jnp.exp(-c)
        e_fin = jnp.exp(c_tot - c)
        kt = stack(kk * jnp.exp(c - lw))
        rt = stack(r_ref[sl, :] * jnp.exp(c))
        bb = stack(beta * e_inv)
        kb = stack(kd * e_inv)
        bh = stack(beta * e_fin)
        kh = stack(kd * e_fin)
        vs = stack(v_ref[sl, :])

        yield
        p1 = _mm_nt(jnp.concatenate([kt, rt], axis=0), jnp.concatenate([bb, kb], axis=0))
        A = p1[0:PAIR, 0:PAIR] * strict
        B = p1[0:PAIR, PAIR:] * strict
        C = p1[PAIR:, 0:PAIR] * incl
        E = p1[PAIR:, PAIR:] * incl
        yield

        d8 = A * m_ref[d, 2]
        d2 = _mm(d8, d8)
        bv = _mm(B, vs)
        yield
        d4 = _mm(d2, d2)
        x = _mm(eye - d8, eye + d2)
        yield
        x = _mm(x, eye + d4)
        yield
        for lvl in range(3):
            aoff = A * m_ref[d, 3 + lvl]
            ax = _mm(aoff, x)
            yield
            x = x - _mm(x, ax)
            yield

        mu = -_mm(x, jnp.concatenate([kt, bv], axis=1))
        ev = _mm(E, vs)
        hv = _mm_tn(kh, vs)
        yield
        cm = _mm(C, mu)
        gh = _mm_tn(bh, mu)
        yield
        rm = rt + cm[:, 0:PAIR]
        y0 = cm[:, PAIR:] + ev
        G = gh[:, 0:PAIR] + eye * jnp.exp(c_tot)
        H = gh[:, PAIR:] + hv
        return rm, y0, G, H

    keys = [(d, j) for d in range(2) for j in range(nsub)]
    gens = [local(d, j) for d, j in keys]
    loc = {}
    while len(loc) < len(keys):
        for key, gen in zip(keys, gens):
            try:
                next(gen)
            except StopIteration as done:
                loc[key] = done.value
    s = [s_ref[0], s_ref[1]]
    for step in range(nsub):
        for d in range(2):
            j = step if d == 0 else nsub - 1 - step
            rm, y0, G, H = loc[d, j]
            yst = _dot3(rm, s[d]) + y0
            dirs[d][6][j * T:(j + 1) * T, :] = yst[0:T, :] + yst[T:, :]
            s[d] = _dot3(G, s[d]) + H
    s_ref[0] = s[0]
    s_ref[1] = s[1]


def _rwkv_scan(r, v, kk, lw, kd, a):
    S = r.shape[0]
    tb = min(256, S)
    nsub = tb // CHUNK
    nblk = S // tb
    npair = RWKV_W // PAIR
    fwd = pl.BlockSpec((tb, PAIR), lambda pr, c: (c, pr))
    bwd = pl.BlockSpec((tb, PAIR), lambda pr, c: (nblk - 1 - c, pr))
    fwd2 = pl.BlockSpec((None, tb, PAIR), lambda pr, c: (0, c, pr))
    bwd2 = pl.BlockSpec((None, tb, PAIR), lambda pr, c: (1, nblk - 1 - c, pr))
    sds = jax.ShapeDtypeStruct((S, RWKV_W), F32)
    return pl.pallas_call(
        functools.partial(_rwkv_scan_kernel, nsub),
        grid=(npair, nblk),
        in_specs=[fwd, fwd, fwd, fwd2, fwd2, fwd2, bwd, bwd, bwd, bwd2, bwd2, bwd2,
                  pl.BlockSpec((2, N_MASK, PAIR, PAIR), lambda pr, c: (0, 0, 0, 0))],
        out_specs=[fwd, bwd],
        out_shape=[sds, sds],
        scratch_shapes=[pltpu.VMEM((2, PAIR, PAIR), F32)],
        compiler_params=_cparams(("parallel", "arbitrary")),
        name="rwkv_scan",
    )(r, v, kk, lw, kd, a, r, v, kk, lw, kd, a, _scan_masks())


def _rwkv_post_kernel(yf_ref, yb_ref, bonus_ref, g_ref, w_ref, b_ref, bd_ref, o_ref):
    y = yf_ref[...] + yb_ref[...]
    bd = bd_ref[...]
    mu = _dot_exactb(y, bd) * (1.0 / RWKV_HEAD)
    yc = y - mu
    var = _dot_exactb(yc * yc, bd) * (1.0 / RWKV_HEAD)
    yn = yc * lax.rsqrt(var + GN_EPS) * w_ref[...] + b_ref[...]
    o_ref[...] = (yn + bonus_ref[...]) * g_ref[...]


def _rwkv_post(yf, yb, bonus, g, lnx_w, lnx_b):
    S = bonus.shape[0]
    tm = min(512, S)
    W = RWKV_W
    row = pl.BlockSpec((tm, W), lambda i: (i, 0))
    return pl.pallas_call(
        _rwkv_post_kernel,
        grid=(S // tm,),
        in_specs=[row, row, row, row,
                  pl.BlockSpec((1, W), lambda i: (0, 0)), pl.BlockSpec((1, W), lambda i: (0, 0)),
                  pl.BlockSpec((W, W), lambda i: (0, 0))],
        out_specs=row,
        out_shape=jax.ShapeDtypeStruct((S, W), F32),
        compiler_params=_cparams(("parallel",)),
        name="rwkv_post",
    )(yf, yb, bonus, g, lnx_w.reshape(1, W), lnx_b.reshape(1, W), _head_blockdiag())


def _lru_kernel(rev, tm, nblk, x_ref, xp_ref, xn_ref, cw_ref, cb_ref, wa_ref, ba_ref, wx_ref, bx_ref, lam_ref,
                o_ref, carry_ref):
    c = pl.program_id(0)
    ib = nblk - 1 - c if rev else c

    @pl.when(c == 0)
    def _():
        carry_ref[...] = jnp.zeros_like(carry_ref)

    x = x_ref[...]
    rows = lax.broadcasted_iota(jnp.int32, (tm, 1), 0)
    prev_row = jnp.where(ib == 0, 0.0, xp_ref[SUBLANES - 1:SUBLANES, :])
    nx0 = jnp.where(ib == nblk - 1, 0.0, xn_ref[0:1, :])
    nx1 = jnp.where(ib == nblk - 1, 0.0, xn_ref[1:2, :])
    xm1 = jnp.where(rows == 0, prev_row, pltpu.roll(x, 1, 0))
    xp1 = jnp.where(rows == tm - 1, nx0, pltpu.roll(x, tm - 1, 0))
    xp2 = jnp.where(rows == tm - 2, nx0, jnp.where(rows == tm - 1, nx1, pltpu.roll(x, tm - 2, 0)))
    xc = (cw_ref[0:1, :] * xm1 + cw_ref[1:2, :] * x + cw_ref[2:3, :] * xp1 + cw_ref[3:4, :] * xp2) + cb_ref[...]
    xcb = xc.astype(BF16)
    ra, ia = [], []
    for g in range(LRU_BLOCKS):
        blk = xcb[:, g * LRU_BLOCK:(g + 1) * LRU_BLOCK]
        ra.append(_dot(blk, wa_ref[g]))
        ia.append(_dot(blk, wx_ref[g]))
    rg = _sigmoid(jnp.concatenate(ra, axis=1) + ba_ref[...])
    ig = _sigmoid(jnp.concatenate(ia, axis=1) + bx_ref[...])
    log_a = -LRU_C * rg * _softplus(-lam_ref[...])
    a = jnp.exp(log_a)
    b = jnp.sqrt(1.0 - jnp.exp(2.0 * log_a)) * (ig * xc)
    k = 1
    while k < tm:
        if rev:
            ap, bp = pltpu.roll(a, tm - k, 0), pltpu.roll(b, tm - k, 0)
            valid = rows < tm - k
        else:
            ap, bp = pltpu.roll(a, k, 0), pltpu.roll(b, k, 0)
            valid = rows >= k
        b = jnp.where(valid, a * bp + b, b)
        a = jnp.where(valid, a * ap, a)
        k *= 2
    h = a * carry_ref[0:1, :] + b
    o_ref[...] = h
    last = h[0:1, :] if rev else h[tm - 1:tm, :]
    carry_ref[...] = jnp.broadcast_to(last, carry_ref.shape)


def _lru_scan(h_lru, l, p, d):
    S = h_lru.shape[0]
    tm = min(512, S)
    nblk = S // tm
    W = LRU_W
    rev = d == 1
    hb = tm // SUBLANES
    nrow8 = S // SUBLANES
    blk = (lambda c: nblk - 1 - c) if rev else (lambda c: c)
    const = lambda shape: pl.BlockSpec(shape, lambda c: (0,) * len(shape))
    return pl.pallas_call(
        functools.partial(_lru_kernel, rev, tm, nblk),
        grid=(nblk,),
        in_specs=[
            pl.BlockSpec((tm, W), lambda c: (blk(c), 0)),
            pl.BlockSpec((SUBLANES, W), lambda c: (jnp.maximum(blk(c) * hb - 1, 0), 0)),
            pl.BlockSpec((SUBLANES, W), lambda c: (jnp.minimum((blk(c) + 1) * hb, nrow8 - 1), 0)),
            const((CONV_W, W)), const((1, W)),
            const((LRU_BLOCKS, LRU_BLOCK, LRU_BLOCK)), const((1, W)),
            const((LRU_BLOCKS, LRU_BLOCK, LRU_BLOCK)), const((1, W)), const((1, W)),
        ],
        out_specs=pl.BlockSpec((tm, W), lambda c: (blk(c), 0)),
        out_shape=jax.ShapeDtypeStruct((S, W), F32),
        scratch_shapes=[pltpu.VMEM((SUBLANES, W), F32)],
        compiler_params=_cparams(("arbitrary",)),
        name="lru_scan_rev" if rev else "lru_scan_fwd",
    )(h_lru, h_lru, h_lru, p['lru_conv_w'][l], p['lru_conv_b'][l].reshape(1, W),
      p['lru_wa'][l, d].astype(BF16), p['lru_ba'][l, d].reshape(1, W),
      p['lru_wx'][l, d].astype(BF16), p['lru_bx'][l, d].reshape(1, W), p['lru_lambda'][l, d].reshape(1, W))


MLA_QK = 2 * LANES
MLA_Q_SCALE = float((QK_NOPE + QK_ROPE) ** -0.5 * np.log2(np.e))


def _mla_prep_kernel(cq_ref, ckv_ref, kr_ref, pos_ref, invf_ref, qn_ref, kvn_ref, wq_ref, wk_ref, wv_ref,
                     q_o, k_o, v_o):
    def norm(x, g):
        ms = jnp.mean(x * x, axis=-1, keepdims=True)
        return (x * lax.rsqrt(ms + NORM_EPS) * g).astype(BF16)

    q = _dot(norm(cq_ref[...], qn_ref[...]), wq_ref[...]) * MLA_Q_SCALE
    ckv = norm(ckv_ref[...], kvn_ref[...])
    kn = _dot(ckv, wk_ref[...])
    v = _dot(ckv, wv_ref[...])
    ang = pos_ref[...].astype(F32) * invf_ref[...]
    lane = lax.broadcasted_iota(jnp.int32, (1, LANES), 1)
    half = QK_ROPE // 2
    cos = jnp.cos(ang)
    sin = jnp.sin(ang)
    s_lo = jnp.where(lane < half, -sin, 0.0)
    s_hi = jnp.where((lane >= half) & (lane < QK_ROPE), sin, 0.0)

    def rope(t):
        return t * cos + pltpu.roll(t, LANES - half, 1) * s_lo + pltpu.roll(t, half, 1) * s_hi

    kpe = rope(kr_ref[...]).astype(BF16)
    for hh in range(MLA_HEADS):
        q_o[hh, :, 0:LANES] = q[:, hh * MLA_QK:hh * MLA_QK + LANES].astype(BF16)
        q_o[hh, :, LANES:] = rope(q[:, hh * MLA_QK + LANES:(hh + 1) * MLA_QK]).astype(BF16)
        k_o[hh, :, 0:LANES] = kn[:, hh * QK_NOPE:(hh + 1) * QK_NOPE].astype(BF16)
        k_o[hh, :, LANES:] = kpe
        v_o[hh, :, 0:V_HEAD] = v[:, hh * V_HEAD:(hh + 1) * V_HEAD].astype(BF16)
        v_o[hh, :, V_HEAD:] = jnp.ones((v.shape[0], V_HEAD), BF16)


def _mla_prep(h_mla, positions, l, p):
    S = h_mla.shape[0]
    tm = min(512, S)
    half = QK_ROPE // 2
    inv = ROPE_THETA ** (-jnp.arange(0, QK_ROPE, 2, dtype=F32) / QK_ROPE)
    invf = jnp.zeros((1, LANES), F32).at[0, :half].set(inv).at[0, half:QK_ROPE].set(inv)
    wq = p['mla_w_uq'][l].reshape(Q_LORA, MLA_HEADS, QK_NOPE + QK_ROPE)
    wq = jnp.concatenate([wq, jnp.zeros((Q_LORA, MLA_HEADS, MLA_QK - QK_NOPE - QK_ROPE), F32)], axis=-1)
    wq = wq.reshape(Q_LORA, MLA_HEADS * MLA_QK).astype(BF16)
    const = lambda shape: pl.BlockSpec(shape, lambda i: (0,) * len(shape))
    hq = jax.ShapeDtypeStruct((MLA_HEADS, S, MLA_QK), BF16)
    return pl.pallas_call(
        _mla_prep_kernel,
        grid=(S // tm,),
        in_specs=[
            pl.BlockSpec((tm, Q_LORA), lambda i: (i, 0)),
            pl.BlockSpec((tm, KV_LORA), lambda i: (i, Q_LORA // KV_LORA)),
            pl.BlockSpec((tm, LANES), lambda i: (i, MLA_COL_KR // LANES)),
            pl.BlockSpec((tm, 1), lambda i: (i, 0)),
            const((1, LANES)), const((1, Q_LORA)), const((1, KV_LORA)),
            const((Q_LORA, MLA_HEADS * MLA_QK)), const((KV_LORA, MLA_HEADS * QK_NOPE)),
            const((KV_LORA, MLA_HEADS * V_HEAD)),
        ],
        out_specs=[pl.BlockSpec((MLA_HEADS, tm, MLA_QK), lambda i: (0, i, 0)),
                   pl.BlockSpec((MLA_HEADS, tm, MLA_QK), lambda i: (0, i, 0)),
                   pl.BlockSpec((MLA_HEADS, tm, 2 * V_HEAD), lambda i: (0, i, 0))],
        out_shape=[hq, hq, jax.ShapeDtypeStruct((MLA_HEADS, S, 2 * V_HEAD), BF16)],
        compiler_params=_cparams(("parallel",)),
        name="mla_prep",
    )(h_mla, h_mla, h_mla, positions.reshape(S, 1), invf, p['mla_q_norm'][l].reshape(1, Q_LORA),
      p['mla_kv_norm'][l].reshape(1, KV_LORA), wq, p['mla_w_uk'][l].astype(BF16), p['mla_w_uv'][l].astype(BF16))


FLASH_ROWS = 8


def _flash_kernel(nkv, tq, q_ref, k_ref, v_ref, o_ref, m_ref, acc_ref):
    j = pl.program_id(2)

    @pl.when(j == 0)
    def _():
        m_ref[...] = jnp.full_like(m_ref, -jnp.inf)
        acc_ref[...] = jnp.zeros_like(acc_ref)

    tr = tq // FLASH_ROWS
    k = k_ref[...]
    v = v_ref[...]
    scores = [_dot_nt(q_ref[g * tr:(g + 1) * tr, :], k) for g in range(FLASH_ROWS)]
    for g in range(FLASH_ROWS):
        rows = slice(g * tr, (g + 1) * tr)
        s = scores[g]
        m_old = m_ref[rows, :]
        m_new = jnp.maximum(m_old, jnp.max(s, axis=-1, keepdims=True))
        pm = jnp.exp2(s - m_new).astype(BF16)
        acc_ref[rows, :] = jnp.exp2(m_old - m_new) * acc_ref[rows, :] + _dot(pm, v)
        m_ref[rows, :] = m_new

    @pl.when(j == nkv - 1)
    def _():
        o_ref[...] = acc_ref[:, 0:V_HEAD] / acc_ref[:, V_HEAD:]


def _flash_attention(q, k, v):
    S = q.shape[1]
    tq = min(2048, S)
    tk = min(2048, S)
    nkv = S // tk
    return pl.pallas_call(
        functools.partial(_flash_kernel, nkv, tq),
        grid=(MLA_HEADS, S // tq, nkv),
        in_specs=[pl.BlockSpec((None, tq, MLA_QK), lambda h, i, j: (h, i, 0)),
                  pl.BlockSpec((None, tk, MLA_QK), lambda h, i, j: (h, j, 0)),
                  pl.BlockSpec((None, tk, 2 * V_HEAD), lambda h, i, j: (h, j, 0))],
        out_specs=pl.BlockSpec((tq, V_HEAD), lambda h, i, j: (i, h)),
        out_shape=jax.ShapeDtypeStruct((S, MLA_W), F32),
        scratch_shapes=[pltpu.VMEM((tq, 1), F32), pltpu.VMEM((tq, 2 * V_HEAD), F32)],
        compiler_params=_cparams(("parallel", "parallel", "arbitrary")),
        name="mla_flash",
    )(q, k, v)


def _merge_kernel(xn_ref, ya_ref, h0_ref, h1_ref, gb_ref, yc_ref, wg0_ref, wg1_ref, wg2_ref,
                  wa_ref, wb_ref, wc_ref, o_ref):
    xn = xn_ref[...]
    yb = (h0_ref[...] + h1_ref[...]) * jax.nn.gelu(gb_ref[...])
    m = _sigmoid(_dot(xn, wg0_ref[...])) * _dot(ya_ref[...].astype(BF16), wa_ref[...])
    m = m + _sigmoid(_dot(xn, wg1_ref[...])) * _dot(yb.astype(BF16), wb_ref[...])
    m = m + _sigmoid(_dot(xn, wg2_ref[...])) * _dot(yc_ref[...].astype(BF16), wc_ref[...])
    o_ref[...] = m.astype(o_ref.dtype)


def _merge(xn, ya, hs0, hs1, h_lru, yc, w_gates, l, p):
    S = ya.shape[0]
    tm = min(512, S)
    tn = 512
    nj = D_MODEL // tn
    rowa = pl.BlockSpec((tm, RWKV_W), lambda j, i: (i, 0))
    wgate = lambda b: pl.BlockSpec((D_MODEL, tn), lambda j, i: (0, b * nj + j))
    wcol = lambda k: pl.BlockSpec((k, tn), lambda j, i: (0, j))
    return pl.pallas_call(
        _merge_kernel,
        grid=(nj, S // tm),
        in_specs=[pl.BlockSpec((tm, D_MODEL), lambda j, i: (i, 0)),
                  rowa, rowa, rowa, pl.BlockSpec((tm, LRU_W), lambda j, i: (i, 1)),
                  pl.BlockSpec((tm, MLA_W), lambda j, i: (i, 0)),
                  wgate(0), wgate(1), wgate(2), wcol(RWKV_W), wcol(LRU_W), wcol(MLA_W)],
        out_specs=pl.BlockSpec((tm, tn), lambda j, i: (i, j)),
        out_shape=jax.ShapeDtypeStruct((S, D_MODEL), BF16),
        compiler_params=_cparams(("parallel", "parallel")),
        name="merge",
    )(xn, ya, hs0, hs1, h_lru, yc, w_gates, w_gates, w_gates,
      p['wo_rwkv'][l].astype(BF16), p['wo_lru'][l].astype(BF16), p['wo_mla'][l].astype(BF16))


def _router_kernel(x_ref, g_ref, wr_ref, xn_o, lg_o):
    x = x_ref[...]
    ms = jnp.mean(x * x, axis=-1, keepdims=True)
    xn = x * lax.rsqrt(ms + NORM_EPS) * g_ref[...]
    xn_o[...] = xn
    lg_o[...] = _dot3_nt(wr_ref[...], xn)


def _router(x, gain, w_router):
    S = x.shape[0]
    tm = min(512, S)
    return pl.pallas_call(
        _router_kernel,
        grid=(S // tm,),
        in_specs=[pl.BlockSpec((tm, D_MODEL), lambda i: (i, 0)), pl.BlockSpec((1, D_MODEL), lambda i: (0, 0)),
                  pl.BlockSpec((N_EXPERTS, D_MODEL), lambda i: (0, 0))],
        out_specs=[pl.BlockSpec((tm, D_MODEL), lambda i: (i, 0)), pl.BlockSpec((N_EXPERTS, tm), lambda i: (0, i))],
        out_shape=[jax.ShapeDtypeStruct((S, D_MODEL), F32), jax.ShapeDtypeStruct((N_EXPERTS, S), F32)],
        compiler_params=_cparams(("parallel",)),
        name="moe_router",
    )(x, gain.reshape(1, D_MODEL), w_router.T)


def _route_kernel(nb, cap, lg_ref, idx_o, gate_o, aff_ref):
    E = N_EXPERTS
    m = lg_ref[0]
    for e in range(1, E):
        m = jnp.maximum(m, lg_ref[e])
    z = jnp.zeros_like(m)
    for e in range(E):
        z = z + jnp.exp(lg_ref[e] - m)
    for e in range(E):
        aff_ref[e] = jnp.exp(lg_ref[e] - m) / z

    li = lax.broadcasted_iota(jnp.int32, (LANES, LANES), 0)
    lj = lax.broadcasted_iota(jnp.int32, (LANES, LANES), 1)
    upper = jnp.where(li <= lj, 1.0, 0.0).astype(BF16)
    bi = lax.broadcasted_iota(jnp.int32, (nb, nb), 0)
    bj = lax.broadcasted_iota(jnp.int32, (nb, nb), 1)
    lstrict = jnp.where(bj < bi, 1.0, 0.0).astype(BF16)
    qrow = lax.broadcasted_iota(jnp.int32, (1, cap), 1).astype(F32)
    brow = lax.broadcasted_iota(jnp.int32, (nb, cap), 0).astype(F32)
    irow = lax.broadcasted_iota(jnp.int32, (LANES, cap), 0).astype(F32)

    def prefix(mask):
        cum = _dot(mask.astype(BF16), upper)
        tot = cum[:, LANES - 1:LANES]
        off = _dot(lstrict, jnp.broadcast_to(tot, (nb, LANES)).astype(BF16))
        return cum, off, tot

    def expert(e, carry):
        aff = aff_ref[e]
        bits = pltpu.bitcast(aff, jnp.int32)

        def bisect(_, lohi):
            lo, hi = lohi
            mid = lo + lax.shift_right_arithmetic(hi - lo, 1)
            cnt = jnp.sum(jnp.where(bits >= mid, 1.0, 0.0), keepdims=True)
            ok = cnt >= cap
            return jnp.where(ok, mid, lo), jnp.where(ok, hi, mid)

        lo0 = jnp.zeros((1, 1), jnp.int32)
        hi0 = jnp.full((1, 1), 0x7F800000, jnp.int32)
        thr, _ = lax.fori_loop(0, 31, bisect, (lo0, hi0))
        gt = jnp.where(bits > thr, 1.0, 0.0)
        eq = jnp.where(bits == thr, 1.0, 0.0)
        need = cap - jnp.sum(gt, keepdims=True)
        ecum, eoff, _ = prefix(eq)
        sel = gt + eq * jnp.where(ecum + eoff <= need, 1.0, 0.0)
        cum, off, tot = prefix(sel)
        full = jnp.where(off[:, 0:1] + tot <= qrow, 1.0, 0.0)
        nfull = jnp.sum(full, axis=0, keepdims=True)
        offsel = jnp.sum(full * tot, axis=0, keepdims=True)
        onehot_b = jnp.where(brow == nfull, 1.0, 0.0).astype(BF16)
        cum_t = _dot_tn(cum.astype(BF16), onehot_b)
        within = jnp.sum(jnp.where(cum_t <= qrow - offsel, 1.0, 0.0), axis=0, keepdims=True)
        idx_o[e] = (nfull * LANES + within).astype(jnp.int32)
        a1 = aff.astype(BF16)
        r1 = aff - a1.astype(F32)
        a2 = r1.astype(BF16)
        a3 = (r1 - a2.astype(F32)).astype(BF16)
        aff_t = _dot_tn(a1, onehot_b) + _dot_tn(a2, onehot_b) + _dot_tn(a3, onehot_b)
        gate_o[e] = jnp.sum(jnp.where(irow == within, aff_t, 0.0), axis=0, keepdims=True)
        return carry

    lax.fori_loop(0, E, expert, 0)


def _route(logits_t, cap):
    E, S = logits_t.shape
    nb = S // LANES
    return pl.pallas_call(
        functools.partial(_route_kernel, nb, cap),
        out_shape=[jax.ShapeDtypeStruct((E, 1, cap), jnp.int32), jax.ShapeDtypeStruct((E, 1, cap), F32)],
        scratch_shapes=[pltpu.VMEM((E, nb, LANES), F32)],
        compiler_params=pltpu.CompilerParams(vmem_limit_bytes=VMEM_LIMIT),
        name="moe_route",
    )(logits_t.reshape(E, nb, LANES))


DMA_UNROLL = 8


def _expert_kernel(tq, nsteps, idx_ref, xacc_hbm, xn_hbm, gate_ref, wg_ref, wu_ref, wd_ref, out_hbm,
                   xs_ref, ob_ref, tok_sems, row_sem, sct_sem):
    del xacc_hbm
    n = pl.program_id(0) * pl.num_programs(1) + pl.program_id(1)
    slot = lax.rem(n, 2)

    def hbm_row(ref, tok):
        return ref.at[lax.shift_right_logical(tok, 3), pl.ds(lax.bitwise_and(tok, SUBLANES - 1), 1), :]

    def token_copy(step, buf, i, u):
        tok = idx_ref[step * tq + i * SUBLANES + u]
        return pltpu.make_async_copy(hbm_row(xn_hbm, tok), xs_ref.at[buf, i, pl.ds(u, 1), :], tok_sems.at[buf])

    def row_copy(step, i, u):
        tok = idx_ref[step * tq + i * SUBLANES + u]
        return pltpu.make_async_copy(hbm_row(out_hbm, tok), ob_ref.at[i, pl.ds(u, 1), :], row_sem.at[0])

    def scatter_copy(step, i, u):
        tok = idx_ref[step * tq + i * SUBLANES + u]
        return pltpu.make_async_copy(ob_ref.at[i, pl.ds(u, 1), :], hbm_row(out_hbm, tok), sct_sem.at[0])

    def for_rows(fn):
        def body(i, c):
            for u in range(SUBLANES):
                fn(i, u)
            return c
        lax.fori_loop(0, tq // SUBLANES, body, 0)

    @pl.when(n == 0)
    def _():
        for_rows(lambda i, u: token_copy(0, 0, i, u).start())

    @pl.when(n + 1 < nsteps)
    def _():
        for_rows(lambda i, u: token_copy(n + 1, 1 - slot, i, u).start())

    for_rows(lambda i, u: token_copy(n, slot, i, u).wait())
    xs = xs_ref[slot].reshape(tq, D_MODEL).astype(BF16)
    hg = _dot(xs, wg_ref[...])
    hu = _dot(xs, wu_ref[...])
    hdn = ((hg * _sigmoid(hg)) * hu).astype(BF16)

    @pl.when(n > 0)
    def _():
        for_rows(lambda i, u: scatter_copy(n - 1, i, u).wait())

    for_rows(lambda i, u: row_copy(n, i, u).start())
    ye = _dot(hdn, wd_ref[...])
    for_rows(lambda i, u: row_copy(n, i, u).wait())
    upd = ob_ref[...].reshape(tq, D_MODEL) + ye * gate_ref[...]
    ob_ref[...] = upd.reshape(tq // SUBLANES, SUBLANES, D_MODEL)
    for_rows(lambda i, u: scatter_copy(n, i, u).start())

    @pl.when(n == nsteps - 1)
    def _():
        for_rows(lambda i, u: scatter_copy(n, i, u).wait())


def _experts(x, xn, idx, gate, l, p):
    S = x.shape[0]
    E, cap = idx.shape
    tq = min(512, cap)
    nt = cap // tq
    grid_spec = pltpu.PrefetchScalarGridSpec(
        num_scalar_prefetch=1,
        grid=(E, nt),
        in_specs=[
            pl.BlockSpec(memory_space=pl.ANY),
            pl.BlockSpec(memory_space=pl.ANY),
            pl.BlockSpec((None, tq, 1), lambda e, t, idx: (e, t, 0)),
            pl.BlockSpec((None, D_MODEL, EXPERT_FF), lambda e, t, idx: (e, 0, 0)),
            pl.BlockSpec((None, D_MODEL, EXPERT_FF), lambda e, t, idx: (e, 0, 0)),
            pl.BlockSpec((None, EXPERT_FF, D_MODEL), lambda e, t, idx: (e, 0, 0)),
        ],
        out_specs=pl.BlockSpec(memory_space=pl.ANY),
        scratch_shapes=[pltpu.VMEM((2, tq // SUBLANES, SUBLANES, D_MODEL), F32),
                        pltpu.VMEM((tq // SUBLANES, SUBLANES, D_MODEL), F32),
                        pltpu.SemaphoreType.DMA((2,)), pltpu.SemaphoreType.DMA((1,)),
                        pltpu.SemaphoreType.DMA((1,))],
    )
    grouped = (S // SUBLANES, SUBLANES, D_MODEL)
    out = pl.pallas_call(
        functools.partial(_expert_kernel, tq, E * nt),
        grid_spec=grid_spec,
        out_shape=jax.ShapeDtypeStruct(grouped, F32),
        input_output_aliases={1: 0},
        compiler_params=_cparams(("arbitrary", "arbitrary")),
        name="moe_experts",
    )(idx.reshape(E * cap), x.reshape(grouped), xn.reshape(grouped), gate.reshape(E, cap, 1),
      p['w_gate'][l].astype(BF16), p['w_up'][l].astype(BF16), p['w_down'][l].astype(BF16))
    return out.reshape(S, D_MODEL)


def _trunk(x, positions, p):
    S = x.shape[0]
    cap = CAPACITY_FACTOR * S // N_EXPERTS
    depth = p['w_in'].shape[0]
    v_first = None
    for l in range(depth):
        xn = _rmsnorm(x, p['norm_mix'][l], BF16)
        h_rwkv, h_lru, h_mla, w_gates = _in_projection(xn, p['w_in'][l], p['rwkv_vres_down'][l - 1] if l > 0 else None)
        r, v, kk, lw, kd, a, g, bonus = _rwkv_prep(h_rwkv, h_mla, l, p, v_first)
        if l == 0:
            v_first = v
        yf, yb = _rwkv_scan(r, v, kk, lw, kd, a)
        ya = _rwkv_post(yf, yb, bonus, g, p['rwkv_lnx_w'][l], p['rwkv_lnx_b'][l])
        hs0 = _lru_scan(h_lru, l, p, 0)
        hs1 = _lru_scan(h_lru, l, p, 1)
        q, k, vv = _mla_prep(h_mla, positions, l, p)
        yc = _flash_attention(q, k, vv)
        merged = _merge(xn, ya, hs0, hs1, h_lru, yc, w_gates, l, p)
        x = _matmul(merged, p['w_out'][l].astype(BF16), D_MODEL, res=x, name="outproj")
        xn2, logits_t = _router(x, p['norm_ffn'][l], p['w_router'][l])
        idx, gate = _route(logits_t, cap)
        x = _experts(x, xn2, idx.reshape(N_EXPERTS, cap), gate.reshape(N_EXPERTS, cap), l, p)
    return _rmsnorm(x, p['norm_final'], F32)


def kernel(x, positions, norm_mix, w_in, rwkv_shift_mu, rwkv_w0, rwkv_w2, rwkv_a0, rwkv_a2, rwkv_g2, rwkv_k_k,
           rwkv_k_a, rwkv_r_k, rwkv_lnx_w, rwkv_lnx_b, rwkv_vres_down, rwkv_vres_up, rwkv_vres_b, lru_conv_w,
           lru_conv_b, lru_wa, lru_ba, lru_wx, lru_bx, lru_lambda, mla_q_norm, mla_kv_norm, mla_w_uq, mla_w_uk,
           mla_w_uv, wo_rwkv, wo_lru, wo_mla, w_out, norm_ffn, w_router, w_gate, w_up, w_down, norm_final):
    p = dict(norm_mix=norm_mix, w_in=w_in, rwkv_shift_mu=rwkv_shift_mu, rwkv_w0=rwkv_w0, rwkv_w2=rwkv_w2,
             rwkv_a0=rwkv_a0, rwkv_a2=rwkv_a2, rwkv_g2=rwkv_g2, rwkv_k_k=rwkv_k_k, rwkv_k_a=rwkv_k_a,
             rwkv_r_k=rwkv_r_k, rwkv_lnx_w=rwkv_lnx_w, rwkv_lnx_b=rwkv_lnx_b, rwkv_vres_down=rwkv_vres_down,
             rwkv_vres_up=rwkv_vres_up, rwkv_vres_b=rwkv_vres_b, lru_conv_w=lru_conv_w, lru_conv_b=lru_conv_b,
             lru_wa=lru_wa, lru_ba=lru_ba, lru_wx=lru_wx, lru_bx=lru_bx, lru_lambda=lru_lambda,
             mla_q_norm=mla_q_norm, mla_kv_norm=mla_kv_norm, mla_w_uq=mla_w_uq, mla_w_uk=mla_w_uk,
             mla_w_uv=mla_w_uv, wo_rwkv=wo_rwkv, wo_lru=wo_lru, wo_mla=wo_mla, w_out=w_out, norm_ffn=norm_ffn,
             w_router=w_router, w_gate=w_gate, w_up=w_up, w_down=w_down, norm_final=norm_final)
    B, S, D = x.shape
    outs = [_trunk(x[b], positions[b], p) for b in range(B)]
    return jnp.stack(outs, axis=0)
```

```python
import functools

import numpy as np
import jax
import jax.numpy as jnp
from jax import lax
from jax.experimental import pallas as pl
from jax.experimental.pallas import tpu as pltpu

F32 = jnp.float32
BF16 = jnp.bfloat16

D_MODEL = 2048
RWKV_HEADS = 12
RWKV_HEAD = 64
RWKV_W = RWKV_HEADS * RWKV_HEAD
DECAY_LORA = 64
ICLR_LORA = 64
VRES_LORA = 32
GATE_LORA = 128
GN_EPS = 64e-5
LRU_BLOCKS = 6
LRU_BLOCK = 128
LRU_W = LRU_BLOCKS * LRU_BLOCK
CONV_W = 4
LRU_C = 8.0
MLA_HEADS = 4
QK_NOPE = 128
QK_ROPE = 64
V_HEAD = 128
Q_LORA = 384
KV_LORA = 128
MLA_W = MLA_HEADS * V_HEAD
ROPE_THETA = 10000.0
N_BRANCH = 3
N_EXPERTS = 16
EXPERT_FF = 1024
CAPACITY_FACTOR = 2
NORM_EPS = 1e-6
RWKV_COLS = 3 * RWKV_W + 2 * DECAY_LORA + 2 * ICLR_LORA + GATE_LORA

LANES = 128
SUBLANES = 8
VMEM_LIMIT = 56 * 1024 * 1024

SEG_LRU = RWKV_COLS
SEG_MLA = SEG_LRU + 2 * LRU_W
SEG_GATES = SEG_MLA + Q_LORA + KV_LORA + QK_ROPE
MLA_SEG_W = Q_LORA + KV_LORA + 2 * LANES
MLA_COL_KR = Q_LORA + KV_LORA
MLA_COL_VRES = MLA_COL_KR + LANES

CHUNK = 64
PAIR = 2 * RWKV_HEAD


def _cparams(sem):
    return pltpu.CompilerParams(dimension_semantics=sem, vmem_limit_bytes=VMEM_LIMIT)


def _dot(a, b):
    return jnp.dot(a, b, preferred_element_type=F32)


def _dot_nt(a, b):
    return lax.dot_general(a, b, (((1,), (1,)), ((), ())), preferred_element_type=F32)


def _dot_tn(a, b):
    return lax.dot_general(a, b, (((0,), (0,)), ((), ())), preferred_element_type=F32)


def _split(x):
    hi = x.astype(BF16)
    lo = (x - hi.astype(F32)).astype(BF16)
    return hi, lo


def _dot_exactb(x, b):
    hi, lo = _split(x)
    return _dot(hi, b) + _dot(lo, b)


def _dot3(a, b):
    ah, al = _split(a)
    bh, bl = _split(b)
    return _dot(ah, bh) + (_dot(ah, bl) + _dot(al, bh))


def _dot3_nt(a, b):
    ah, al = _split(a)
    bh, bl = _split(b)
    return _dot_nt(ah, bh) + (_dot_nt(ah, bl) + _dot_nt(al, bh))


def _dot3_tn(a, b):
    ah, al = _split(a)
    bh, bl = _split(b)
    return _dot_tn(ah, bh) + (_dot_tn(ah, bl) + _dot_tn(al, bh))


def _sigmoid(x):
    return 1.0 / (1.0 + jnp.exp(-x))


def _softplus(x):
    return jnp.maximum(x, 0.0) + jnp.log(1.0 + jnp.exp(-jnp.abs(x)))


def _rmsnorm_kernel(x_ref, g_ref, o_ref):
    x = x_ref[...]
    ms = jnp.mean(x * x, axis=-1, keepdims=True)
    o_ref[...] = (x * lax.rsqrt(ms + NORM_EPS) * g_ref[...]).astype(o_ref.dtype)


def _rmsnorm(x, gain, dtype):
    S, D = x.shape
    tm = min(512, S)
    return pl.pallas_call(
        _rmsnorm_kernel,
        grid=(S // tm,),
        in_specs=[pl.BlockSpec((tm, D), lambda i: (i, 0)), pl.BlockSpec((1, D), lambda i: (0, 0))],
        out_specs=pl.BlockSpec((tm, D), lambda i: (i, 0)),
        out_shape=jax.ShapeDtypeStruct((S, D), dtype),
        compiler_params=_cparams(("parallel",)),
        name="rmsnorm",
    )(x, gain.reshape(1, D))


def _matmul_kernel(has_res, *refs):
    if has_res:
        a_ref, w_ref, res_ref, o_ref = refs
        o_ref[...] = res_ref[...] + _dot(a_ref[...], w_ref[...])
    else:
        a_ref, w_ref, o_ref = refs
        o_ref[...] = _dot(a_ref[...], w_ref[...])


def _matmul(a, w, tn, res=None, name="matmul"):
    M, K = a.shape
    N = w.shape[1]
    tm = min(512, M)
    in_specs = [pl.BlockSpec((tm, K), lambda j, i: (i, 0)), pl.BlockSpec((K, tn), lambda j, i: (0, j))]
    args = [a, w]
    if res is not None:
        in_specs.append(pl.BlockSpec((tm, tn), lambda j, i: (i, j)))
        args.append(res)
    return pl.pallas_call(
        functools.partial(_matmul_kernel, res is not None),
        grid=(N // tn, M // tm),
        in_specs=in_specs,
        out_specs=pl.BlockSpec((tm, tn), lambda j, i: (i, j)),
        out_shape=jax.ShapeDtypeStruct((M, N), F32),
        compiler_params=_cparams(("parallel", "parallel")),
        name=name,
    )(*args)


def _in_projection(xn, w_in, vres_down):
    w = w_in.astype(BF16)
    w_mla = jnp.zeros((D_MODEL, MLA_SEG_W), BF16)
    w_mla = w_mla.at[:, :MLA_COL_KR + QK_ROPE].set(w[:, SEG_MLA:SEG_GATES])
    if vres_down is not None:
        w_mla = w_mla.at[:, MLA_COL_VRES:MLA_COL_VRES + VRES_LORA].set(vres_down.astype(BF16))
    h_rwkv = _matmul(xn, w[:, :SEG_LRU], RWKV_COLS, name="inproj_rwkv")
    h_lru = _matmul(xn, w[:, SEG_LRU:SEG_MLA], 2 * LRU_W, name="inproj_lru")
    h_mla = _matmul(xn, w_mla, MLA_SEG_W, name="inproj_mla")
    return h_rwkv, h_lru, h_mla, w[:, SEG_GATES:]


def _rwkv_prep_kernel(has_vres, tm, nblk, *refs):
    if has_vres:
        (h_ref, hp_ref, hn_ref, lora_ref, vfirst_ref, vup_ref, vb_ref, *rest) = refs
    else:
        (h_ref, hp_ref, hn_ref, *rest) = refs
    (mu_ref, w0_ref, w2_ref, a0_ref, a2_ref, g2_ref, kkw_ref, ka_ref, rk_ref, bd_ref,
     r_o, v_o, kk_o, lw_o, kd_o, a_o, g_o, bonus_o) = rest
    i = pl.program_id(0)
    p = h_ref[...]
    rows = lax.broadcasted_iota(jnp.int32, (tm, 1), 0)
    prev_row = jnp.where(i == 0, 0.0, hp_ref[SUBLANES - 1:SUBLANES, :])
    next_row = jnp.where(i == nblk - 1, 0.0, hn_ref[0:1, :])
    prev = jnp.where(rows == 0, prev_row, pltpu.roll(p, 1, 0))
    nxt = jnp.where(rows == tm - 1, next_row, pltpu.roll(p, tm - 1, 0))
    hs = p + mu_ref[0:1, :] * (prev - p) + mu_ref[1:2, :] * (nxt - p)

    W = RWKV_W
    r = hs[:, 0:W]
    k = hs[:, W:2 * W]
    v = hs[:, 2 * W:3 * W]
    wd = hs[:, 3 * W:3 * W + LANES]
    ad = hs[:, 3 * W + LANES:3 * W + 2 * LANES]
    gd = hs[:, 3 * W + 2 * LANES:3 * W + 3 * LANES]
    if has_vres:
        vg = _sigmoid(vb_ref[...] + _dot(lora_ref[...].astype(BF16), vup_ref[...]))
        v = v + (vfirst_ref[...] - v) * vg
    bd = bd_ref[...]
    kk = k * kkw_ref[...]
    n2 = _dot_exactb(kk * kk, bd)
    kk = kk / jnp.maximum(jnp.sqrt(n2), 1e-12)
    tw = jnp.tanh(wd).astype(BF16)
    adb = ad.astype(BF16)
    ksum = None
    for d in range(2):
        wl = w0_ref[d:d + 1, :] + _dot(tw, w2_ref[d])
        w_log = -_softplus(-wl) - 0.5
        lw_o[d] = -jnp.exp(w_log)
        a = _sigmoid(a0_ref[d:d + 1, :] + _dot(adb, a2_ref[d]))
        kd = k * (1.0 + (a - 1.0) * ka_ref[...])
        a_o[d] = a
        kd_o[d] = kd
        ksum = kd if ksum is None else ksum + kd
    r_o[...] = r
    v_o[...] = v
    kk_o[...] = kk
    g_o[...] = _dot(_sigmoid(gd).astype(BF16), g2_ref[...])
    bonus_o[...] = _dot_exactb(r * ksum * rk_ref[...], bd) * v


def _head_blockdiag():
    hid = np.arange(RWKV_W) // RWKV_HEAD
    return jnp.asarray((hid[:, None] == hid[None, :]).astype(np.float32), dtype=BF16)


def _rwkv_prep(h, h_mla, l, p, v_first):
    S = h.shape[0]
    tm = min(256, S)
    nblk = S // tm
    has_vres = l > 0
    W = RWKV_W
    hb = tm // SUBLANES
    nrow8 = S // SUBLANES

    def zpad(w, rows_before):
        z = jnp.zeros((LANES, W), F32)
        return z.at[rows_before:rows_before + w.shape[0]].set(w)

    w2 = jnp.stack([zpad(p['rwkv_w2'][l, 0], 0), zpad(p['rwkv_w2'][l, 1], DECAY_LORA)]).astype(BF16)
    a2 = jnp.stack([zpad(p['rwkv_a2'][l, 0], 0), zpad(p['rwkv_a2'][l, 1], ICLR_LORA)]).astype(BF16)
    const2 = lambda shape: pl.BlockSpec(shape, lambda i: (0,) * len(shape))
    in_specs = [
        pl.BlockSpec((tm, RWKV_COLS), lambda i: (i, 0)),
        pl.BlockSpec((SUBLANES, RWKV_COLS), lambda i: (jnp.maximum(i * hb - 1, 0), 0)),
        pl.BlockSpec((SUBLANES, RWKV_COLS), lambda i: (jnp.minimum((i + 1) * hb, nrow8 - 1), 0)),
    ]
    args = [h, h, h]
    if has_vres:
        in_specs += [
            pl.BlockSpec((tm, LANES), lambda i: (i, MLA_COL_VRES // LANES)),
            pl.BlockSpec((tm, W), lambda i: (i, 0)),
            const2((LANES, W)),
            const2((1, W)),
        ]
        vup = jnp.zeros((LANES, W), F32).at[:VRES_LORA].set(p['rwkv_vres_up'][l - 1]).astype(BF16)
        args += [h_mla, v_first, vup, p['rwkv_vres_b'][l - 1].reshape(1, W)]
    in_specs += [
        const2((2, RWKV_COLS)), const2((2, W)), const2((2, LANES, W)), const2((2, W)), const2((2, LANES, W)),
        const2((LANES, W)), const2((1, W)), const2((1, W)), const2((1, W)), const2((W, W)),
    ]
    args += [
        p['rwkv_shift_mu'][l], p['rwkv_w0'][l], w2, p['rwkv_a0'][l], a2,
        p['rwkv_g2'][l].astype(BF16), p['rwkv_k_k'][l].reshape(1, W), p['rwkv_k_a'][l].reshape(1, W),
        p['rwkv_r_k'][l].reshape(1, W), _head_blockdiag(),
    ]
    row = pl.BlockSpec((tm, W), lambda i: (i, 0))
    row2 = pl.BlockSpec((2, tm, W), lambda i: (0, i, 0))
    sds = jax.ShapeDtypeStruct((S, W), F32)
    sds2 = jax.ShapeDtypeStruct((2, S, W), F32)
    return pl.pallas_call(
        functools.partial(_rwkv_prep_kernel, has_vres, tm, nblk),
        grid=(nblk,),
        in_specs=in_specs,
        out_specs=[row, row, row, row2, row2, row2, row, row],
        out_shape=[sds, sds, sds, sds2, sds2, sds2, sds, sds],
        compiler_params=_cparams(("parallel",)),
        name="rwkv_prep",
    )(*args)


N_MASK = 8


def _scan_masks():
    n = PAIR
    out = np.zeros((2, N_MASK, n, n), np.float32)
    r = np.arange(n)[:, None]
    c = np.arange(n)[None, :]
    same = (r // CHUNK) == (c // CHUNK)
    for d in range(2):
        before = (c < r) if d == 0 else (c > r)
        strict = same & before
        out[d, 0] = strict
        out[d, 1] = same & (before | (c == r))
        out[d, 2] = strict & ((r // 8) == (c // 8))
        for j, b in enumerate((8, 16, 32)):
            out[d, 3 + j] = strict & ((r // (2 * b)) == (c // (2 * b))) & ((r // b) != (c // b))
        out[d, 6] = (r == c)
        out[d, 7] = (before | (c == r)) & (r < CHUNK) & (c < CHUNK)
    return jnp.asarray(out)


def _mm(a, b):
    return _dot(a.astype(BF16), b.astype(BF16))


def _mm_nt(a, b):
    return _dot_nt(a.astype(BF16), b.astype(BF16))


def _mm_tn(a, b):
    return _dot_tn(a.astype(BF16), b.astype(BF16))


def _rwkv_scan_kernel(nsub, rf, vf, kkf, lwf, kf, af, rb, vb, kkb, lwb, kb_, ab, m_ref, yf, yb, s_ref):
    T = CHUNK
    dirs = ((rf, vf, kkf, lwf, kf, af, yf), (rb, vb, kkb, lwb, kb_, ab, yb))

    @pl.when(pl.program_id(1) == 0)
    def _():
        s_ref[...] = jnp.zeros_like(s_ref)

    lane = lax.broadcasted_iota(jnp.int32, (1, PAIR), 1)
    m0 = jnp.where(lane < RWKV_HEAD, 1.0, 0.0)
    m1 = 1.0 - m0

    def stack(x):
        return jnp.concatenate([x * m0, x * m1], axis=0)

    def local(d, j):
        r_ref, v_ref, kk_ref, lw_ref, k_ref, a_ref, _ = dirs[d]
        sl = slice(j * T, (j + 1) * T)
        strict, incl, eye = m_ref[d, 0], m_ref[d, 1], m_ref[d, 6]
        lw = lw_ref[sl, :]
        kk = kk_ref[sl, :]
        beta = a_ref[sl, :] * kk
        kd = k_ref[sl, :]
        cs = m_ref[d, 7][0:T, 0:T].astype(BF16)
        lh, ll = _split(lw)
        c = _dot(cs, lh) + _dot(cs, ll)
        c_tot = c[0:1, :] if d == 1 else c[T - 1:T, :]
        e_inv = jnp.exp(-c)
        e_fin = jnp.exp(c_tot - c)
        kt = stack(kk * jnp.exp(c - lw))
        rt = stack(r_ref[sl, :] * jnp.exp(c))
        bb = stack(beta * e_inv)
        kb = stack(kd * e_inv)
        bh = stack(beta * e_fin)
        kh = stack(kd * e_fin)
        vs = stack(v_ref[sl, :])

        yield
        p1 = _mm_nt(jnp.concatenate([kt, rt], axis=0), jnp.concatenate([bb, kb], axis=0))
        A = p1[0:PAIR, 0:PAIR] * strict
        B = p1[0:PAIR, PAIR:] * strict
        C = p1[PAIR:, 0:PAIR] * incl
        E = p1[PAIR:, PAIR:] * incl
        yield

        d8 = A * m_ref[d, 2]
        d2 = _mm(d8, d8)
        bv = _mm(B, vs)
        yield
        d4 = _mm(d2, d2)
        x = _mm(eye - d8, eye + d2)
        yield
        x = _mm(x, eye + d4)
        yield
        for lvl in range(3):
            aoff = A * m_ref[d, 3 + lvl]
            ax = _mm(aoff, x)
            yield
            x = x - _mm(x, ax)
            yield

        mu = -_mm(x, jnp.concatenate([kt, bv], axis=1))
        ev = _mm(E, vs)
        hv = _mm_tn(kh, vs)
        yield
        cm = _mm(C, mu)
        gh = _mm_tn(bh, mu)
        yield
        rm = rt + cm[:, 0:PAIR]
        y0 = cm[:, PAIR:] + ev
        p_col = jnp.sum(eye * jnp.exp(c_tot), axis=1, keepdims=True)
        Gp = gh[:, 0:PAIR].astype(BF16)
        H = gh[:, PAIR:] + hv
        return rm.astype(BF16), y0, p_col, Gp, H

    keys = [(d, j) for d in range(2) for j in range(nsub)]
    gens = [local(d, j) for d, j in keys]
    loc = {}
    while len(loc) < len(keys):
        for key, gen in zip(keys, gens):
            try:
                next(gen)
            except StopIteration as done:
                loc[key] = done.value
    s = [s_ref[0], s_ref[1]]
    for step in range(nsub):
        for d in range(2):
            j = step if d == 0 else nsub - 1 - step
            rm, y0, p_col, Gp, H = loc[d, j]
            sb = s[d].astype(BF16)
            yst = _dot(rm, sb) + y0
            dirs[d][6][j * T:(j + 1) * T, :] = yst[0:T, :] + yst[T:, :]
            s[d] = p_col * s[d] + (_dot(Gp, sb) + H)
    s_ref[0] = s[0]
    s_ref[1] = s[1]


def _rwkv_scan(r, v, kk, lw, kd, a):
    S = r.shape[0]
    tb = min(1024, S)
    nsub = tb // CHUNK
    nblk = S // tb
    npair = RWKV_W // PAIR
    fwd = pl.BlockSpec((tb, PAIR), lambda pr, c: (c, pr))
    bwd = pl.BlockSpec((tb, PAIR), lambda pr, c: (nblk - 1 - c, pr))
    fwd2 = pl.BlockSpec((None, tb, PAIR), lambda pr, c: (0, c, pr))
    bwd2 = pl.BlockSpec((None, tb, PAIR), lambda pr, c: (1, nblk - 1 - c, pr))
    sds = jax.ShapeDtypeStruct((S, RWKV_W), F32)
    return pl.pallas_call(
        functools.partial(_rwkv_scan_kernel, nsub),
        grid=(npair, nblk),
        in_specs=[fwd, fwd, fwd, fwd2, fwd2, fwd2, bwd, bwd, bwd, bwd2, bwd2, bwd2,
                  pl.BlockSpec((2, N_MASK, PAIR, PAIR), lambda pr, c: (0, 0, 0, 0))],
        out_specs=[fwd, bwd],
        out_shape=[sds, sds],
        scratch_shapes=[pltpu.VMEM((2, PAIR, PAIR), F32)],
        compiler_params=_cparams(("parallel", "arbitrary")),
        name="rwkv_scan",
    )(r, v, kk, lw, kd, a, r, v, kk, lw, kd, a, _scan_masks())


def _rwkv_post_kernel(yf_ref, yb_ref, bonus_ref, g_ref, w_ref, b_ref, bd_ref, o_ref):
    y = yf_ref[...] + yb_ref[...]
    bd = bd_ref[...]
    mu = _dot_exactb(y, bd) * (1.0 / RWKV_HEAD)
    yc = y - mu
    var = _dot_exactb(yc * yc, bd) * (1.0 / RWKV_HEAD)
    yn = yc * lax.rsqrt(var + GN_EPS) * w_ref[...] + b_ref[...]
    o_ref[...] = (yn + bonus_ref[...]) * g_ref[...]


def _rwkv_post(yf, yb, bonus, g, lnx_w, lnx_b):
    S = bonus.shape[0]
    tm = min(512, S)
    W = RWKV_W
    row = pl.BlockSpec((tm, W), lambda i: (i, 0))
    return pl.pallas_call(
        _rwkv_post_kernel,
        grid=(S // tm,),
        in_specs=[row, row, row, row,
                  pl.BlockSpec((1, W), lambda i: (0, 0)), pl.BlockSpec((1, W), lambda i: (0, 0)),
                  pl.BlockSpec((W, W), lambda i: (0, 0))],
        out_specs=row,
        out_shape=jax.ShapeDtypeStruct((S, W), F32),
        compiler_params=_cparams(("parallel",)),
        name="rwkv_post",
    )(yf, yb, bonus, g, lnx_w.reshape(1, W), lnx_b.reshape(1, W), _head_blockdiag())


def _lru_kernel(rev, tm, nblk, x_ref, xp_ref, xn_ref, cw_ref, cb_ref, wa_ref, ba_ref, wx_ref, bx_ref, lam_ref,
                o_ref, carry_ref):
    c = pl.program_id(0)
    ib = nblk - 1 - c if rev else c

    @pl.when(c == 0)
    def _():
        carry_ref[...] = jnp.zeros_like(carry_ref)

    x = x_ref[...]
    rows = lax.broadcasted_iota(jnp.int32, (tm, 1), 0)
    prev_row = jnp.where(ib == 0, 0.0, xp_ref[SUBLANES - 1:SUBLANES, :])
    nx0 = jnp.where(ib == nblk - 1, 0.0, xn_ref[0:1, :])
    nx1 = jnp.where(ib == nblk - 1, 0.0, xn_ref[1:2, :])
    xm1 = jnp.where(rows == 0, prev_row, pltpu.roll(x, 1, 0))
    xp1 = jnp.where(rows == tm - 1, nx0, pltpu.roll(x, tm - 1, 0))
    xp2 = jnp.where(rows == tm - 2, nx0, jnp.where(rows == tm - 1, nx1, pltpu.roll(x, tm - 2, 0)))
    xc = (cw_ref[0:1, :] * xm1 + cw_ref[1:2, :] * x + cw_ref[2:3, :] * xp1 + cw_ref[3:4, :] * xp2) + cb_ref[...]
    xcb = xc.astype(BF16)
    ra, ia = [], []
    for g in range(LRU_BLOCKS):
        blk = xcb[:, g * LRU_BLOCK:(g + 1) * LRU_BLOCK]
        ra.append(_dot(blk, wa_ref[g]))
        ia.append(_dot(blk, wx_ref[g]))
    rg = _sigmoid(jnp.concatenate(ra, axis=1) + ba_ref[...])
    ig = _sigmoid(jnp.concatenate(ia, axis=1) + bx_ref[...])
    log_a = -LRU_C * rg * _softplus(-lam_ref[...])
    a = jnp.exp(log_a)
    b = jnp.sqrt(1.0 - jnp.exp(2.0 * log_a)) * (ig * xc)
    k = 1
    while k < tm:
        if rev:
            ap, bp = pltpu.roll(a, tm - k, 0), pltpu.roll(b, tm - k, 0)
            valid = rows < tm - k
        else:
            ap, bp = pltpu.roll(a, k, 0), pltpu.roll(b, k, 0)
            valid = rows >= k
        b = jnp.where(valid, a * bp + b, b)
        a = jnp.where(valid, a * ap, a)
        k *= 2
    h = a * carry_ref[0:1, :] + b
    o_ref[...] = h
    last = h[0:1, :] if rev else h[tm - 1:tm, :]
    carry_ref[...] = jnp.broadcast_to(last, carry_ref.shape)


def _lru_scan(h_lru, l, p, d):
    S = h_lru.shape[0]
    tm = min(512, S)
    nblk = S // tm
    W = LRU_W
    rev = d == 1
    hb = tm // SUBLANES
    nrow8 = S // SUBLANES
    blk = (lambda c: nblk - 1 - c) if rev else (lambda c: c)
    const = lambda shape: pl.BlockSpec(shape, lambda c: (0,) * len(shape))
    return pl.pallas_call(
        functools.partial(_lru_kernel, rev, tm, nblk),
        grid=(nblk,),
        in_specs=[
            pl.BlockSpec((tm, W), lambda c: (blk(c), 0)),
            pl.BlockSpec((SUBLANES, W), lambda c: (jnp.maximum(blk(c) * hb - 1, 0), 0)),
            pl.BlockSpec((SUBLANES, W), lambda c: (jnp.minimum((blk(c) + 1) * hb, nrow8 - 1), 0)),
            const((CONV_W, W)), const((1, W)),
            const((LRU_BLOCKS, LRU_BLOCK, LRU_BLOCK)), const((1, W)),
            const((LRU_BLOCKS, LRU_BLOCK, LRU_BLOCK)), const((1, W)), const((1, W)),
        ],
        out_specs=pl.BlockSpec((tm, W), lambda c: (blk(c), 0)),
        out_shape=jax.ShapeDtypeStruct((S, W), F32),
        scratch_shapes=[pltpu.VMEM((SUBLANES, W), F32)],
        compiler_params=_cparams(("arbitrary",)),
        name="lru_scan_rev" if rev else "lru_scan_fwd",
    )(h_lru, h_lru, h_lru, p['lru_conv_w'][l], p['lru_conv_b'][l].reshape(1, W),
      p['lru_wa'][l, d].astype(BF16), p['lru_ba'][l, d].reshape(1, W),
      p['lru_wx'][l, d].astype(BF16), p['lru_bx'][l, d].reshape(1, W), p['lru_lambda'][l, d].reshape(1, W))


MLA_QK = 2 * LANES
MLA_Q_SCALE = float((QK_NOPE + QK_ROPE) ** -0.5 * np.log2(np.e))


def _mla_prep_kernel(cq_ref, ckv_ref, kr_ref, pos_ref, invf_ref, qn_ref, kvn_ref, wq_ref, wk_ref, wv_ref,
                     q_o, k_o, v_o):
    def norm(x, g):
        ms = jnp.mean(x * x, axis=-1, keepdims=True)
        return (x * lax.rsqrt(ms + NORM_EPS) * g).astype(BF16)

    q = _dot(norm(cq_ref[...], qn_ref[...]), wq_ref[...]) * MLA_Q_SCALE
    ckv = norm(ckv_ref[...], kvn_ref[...])
    kn = _dot(ckv, wk_ref[...])
    v = _dot(ckv, wv_ref[...])
    ang = pos_ref[...].astype(F32) * invf_ref[...]
    lane = lax.broadcasted_iota(jnp.int32, (1, LANES), 1)
    half = QK_ROPE // 2
    cos = jnp.cos(ang)
    sin = jnp.sin(ang)
    s_lo = jnp.where(lane < half, -sin, 0.0)
    s_hi = jnp.where((lane >= half) & (lane < QK_ROPE), sin, 0.0)

    def rope(t):
        return t * cos + pltpu.roll(t, LANES - half, 1) * s_lo + pltpu.roll(t, half, 1) * s_hi

    kpe = rope(kr_ref[...]).astype(BF16)
    for hh in range(MLA_HEADS):
        q_o[hh, :, 0:LANES] = q[:, hh * MLA_QK:hh * MLA_QK + LANES].astype(BF16)
        q_o[hh, :, LANES:] = rope(q[:, hh * MLA_QK + LANES:(hh + 1) * MLA_QK]).astype(BF16)
        k_o[hh, :, 0:LANES] = kn[:, hh * QK_NOPE:(hh + 1) * QK_NOPE].astype(BF16)
        k_o[hh, :, LANES:] = kpe
        v_o[hh, :, 0:V_HEAD] = v[:, hh * V_HEAD:(hh + 1) * V_HEAD].astype(BF16)
        v_o[hh, :, V_HEAD:] = jnp.ones((v.shape[0], V_HEAD), BF16)


def _mla_prep(h_mla, positions, l, p):
    S = h_mla.shape[0]
    tm = min(512, S)
    half = QK_ROPE // 2
    inv = ROPE_THETA ** (-jnp.arange(0, QK_ROPE, 2, dtype=F32) / QK_ROPE)
    invf = jnp.zeros((1, LANES), F32).at[0, :half].set(inv).at[0, half:QK_ROPE].set(inv)
    wq = p['mla_w_uq'][l].reshape(Q_LORA, MLA_HEADS, QK_NOPE + QK_ROPE)
    wq = jnp.concatenate([wq, jnp.zeros((Q_LORA, MLA_HEADS, MLA_QK - QK_NOPE - QK_ROPE), F32)], axis=-1)
    wq = wq.reshape(Q_LORA, MLA_HEADS * MLA_QK).astype(BF16)
    const = lambda shape: pl.BlockSpec(shape, lambda i: (0,) * len(shape))
    hq = jax.ShapeDtypeStruct((MLA_HEADS, S, MLA_QK), BF16)
    return pl.pallas_call(
        _mla_prep_kernel,
        grid=(S // tm,),
        in_specs=[
            pl.BlockSpec((tm, Q_LORA), lambda i: (i, 0)),
            pl.BlockSpec((tm, KV_LORA), lambda i: (i, Q_LORA // KV_LORA)),
            pl.BlockSpec((tm, LANES), lambda i: (i, MLA_COL_KR // LANES)),
            pl.BlockSpec((tm, 1), lambda i: (i, 0)),
            const((1, LANES)), const((1, Q_LORA)), const((1, KV_LORA)),
            const((Q_LORA, MLA_HEADS * MLA_QK)), const((KV_LORA, MLA_HEADS * QK_NOPE)),
            const((KV_LORA, MLA_HEADS * V_HEAD)),
        ],
        out_specs=[pl.BlockSpec((MLA_HEADS, tm, MLA_QK), lambda i: (0, i, 0)),
                   pl.BlockSpec((MLA_HEADS, tm, MLA_QK), lambda i: (0, i, 0)),
                   pl.BlockSpec((MLA_HEADS, tm, 2 * V_HEAD), lambda i: (0, i, 0))],
        out_shape=[hq, hq, jax.ShapeDtypeStruct((MLA_HEADS, S, 2 * V_HEAD), BF16)],
        compiler_params=_cparams(("parallel",)),
        name="mla_prep",
    )(h_mla, h_mla, h_mla, positions.reshape(S, 1), invf, p['mla_q_norm'][l].reshape(1, Q_LORA),
      p['mla_kv_norm'][l].reshape(1, KV_LORA), wq, p['mla_w_uk'][l].astype(BF16), p['mla_w_uv'][l].astype(BF16))


FLASH_ROWS = 4


def _flash_kernel(nkv, tq, q_ref, k_ref, v_ref, o_ref, m_ref, acc_ref):
    j = pl.program_id(2)

    @pl.when(j == 0)
    def _():
        m_ref[...] = jnp.full_like(m_ref, -jnp.inf)
        acc_ref[...] = jnp.zeros_like(acc_ref)

    tr = tq // FLASH_ROWS
    k = k_ref[...]
    v = v_ref[...]
    scores = [_dot_nt(q_ref[g * tr:(g + 1) * tr, :], k) for g in range(FLASH_ROWS)]
    for g in range(FLASH_ROWS):
        rows = slice(g * tr, (g + 1) * tr)
        s = scores[g]
        m_old = m_ref[rows, :]
        m_new = jnp.maximum(m_old, jnp.max(s, axis=-1, keepdims=True))
        pm = jnp.exp2(s - m_new).astype(BF16)
        acc_ref[rows, :] = jnp.exp2(m_old - m_new) * acc_ref[rows, :] + _dot(pm, v)
        m_ref[rows, :] = m_new

    @pl.when(j == nkv - 1)
    def _():
        o_ref[...] = acc_ref[:, 0:V_HEAD] / acc_ref[:, V_HEAD:]


def _flash_attention(q, k, v):
    S = q.shape[1]
    tq = min(2048, S)
    tk = min(2048, S)
    nkv = S // tk
    return pl.pallas_call(
        functools.partial(_flash_kernel, nkv, tq),
        grid=(MLA_HEADS, S // tq, nkv),
        in_specs=[pl.BlockSpec((None, tq, MLA_QK), lambda h, i, j: (h, i, 0)),
                  pl.BlockSpec((None, tk, MLA_QK), lambda h, i, j: (h, j, 0)),
                  pl.BlockSpec((None, tk, 2 * V_HEAD), lambda h, i, j: (h, j, 0))],
        out_specs=pl.BlockSpec((tq, V_HEAD), lambda h, i, j: (i, h)),
        out_shape=jax.ShapeDtypeStruct((S, MLA_W), F32),
        scratch_shapes=[pltpu.VMEM((tq, 1), F32), pltpu.VMEM((tq, 2 * V_HEAD), F32)],
        compiler_params=_cparams(("parallel", "parallel", "arbitrary")),
        name="mla_flash",
    )(q, k, v)


def _merge_kernel(xn_ref, ya_ref, h0_ref, h1_ref, gb_ref, yc_ref, wg0_ref, wg1_ref, wg2_ref,
                  wa_ref, wb_ref, wc_ref, o_ref):
    xn = xn_ref[...]
    yb = (h0_ref[...] + h1_ref[...]) * jax.nn.gelu(gb_ref[...])
    m = _sigmoid(_dot(xn, wg0_ref[...])) * _dot(ya_ref[...].astype(BF16), wa_ref[...])
    m = m + _sigmoid(_dot(xn, wg1_ref[...])) * _dot(yb.astype(BF16), wb_ref[...])
    m = m + _sigmoid(_dot(xn, wg2_ref[...])) * _dot(yc_ref[...].astype(BF16), wc_ref[...])
    o_ref[...] = m.astype(o_ref.dtype)


def _merge(xn, ya, hs0, hs1, h_lru, yc, w_gates, l, p):
    S = ya.shape[0]
    tm = min(512, S)
    tn = 512
    nj = D_MODEL // tn
    rowa = pl.BlockSpec((tm, RWKV_W), lambda j, i: (i, 0))
    wgate = lambda b: pl.BlockSpec((D_MODEL, tn), lambda j, i: (0, b * nj + j))
    wcol = lambda k: pl.BlockSpec((k, tn), lambda j, i: (0, j))
    return pl.pallas_call(
        _merge_kernel,
        grid=(nj, S // tm),
        in_specs=[pl.BlockSpec((tm, D_MODEL), lambda j, i: (i, 0)),
                  rowa, rowa, rowa, pl.BlockSpec((tm, LRU_W), lambda j, i: (i, 1)),
                  pl.BlockSpec((tm, MLA_W), lambda j, i: (i, 0)),
                  wgate(0), wgate(1), wgate(2), wcol(RWKV_W), wcol(LRU_W), wcol(MLA_W)],
        out_specs=pl.BlockSpec((tm, tn), lambda j, i: (i, j)),
        out_shape=jax.ShapeDtypeStruct((S, D_MODEL), BF16),
        compiler_params=_cparams(("parallel", "parallel")),
        name="merge",
    )(xn, ya, hs0, hs1, h_lru, yc, w_gates, w_gates, w_gates,
      p['wo_rwkv'][l].astype(BF16), p['wo_lru'][l].astype(BF16), p['wo_mla'][l].astype(BF16))


def _router_kernel(x_ref, g_ref, wr_ref, xn_o, lg_o):
    x = x_ref[...]
    ms = jnp.mean(x * x, axis=-1, keepdims=True)
    xn = x * lax.rsqrt(ms + NORM_EPS) * g_ref[...]
    xn_o[...] = xn
    lg_o[...] = _dot3_nt(wr_ref[...], xn)


def _router(x, gain, w_router):
    S = x.shape[0]
    tm = min(512, S)
    return pl.pallas_call(
        _router_kernel,
        grid=(S // tm,),
        in_specs=[pl.BlockSpec((tm, D_MODEL), lambda i: (i, 0)), pl.BlockSpec((1, D_MODEL), lambda i: (0, 0)),
                  pl.BlockSpec((N_EXPERTS, D_MODEL), lambda i: (0, 0))],
        out_specs=[pl.BlockSpec((tm, D_MODEL), lambda i: (i, 0)), pl.BlockSpec((N_EXPERTS, tm), lambda i: (0, i))],
        out_shape=[jax.ShapeDtypeStruct((S, D_MODEL), F32), jax.ShapeDtypeStruct((N_EXPERTS, S), F32)],
        compiler_params=_cparams(("parallel",)),
        name="moe_router",
    )(x, gain.reshape(1, D_MODEL), w_router.T)


def _route_kernel(nb, cap, lg_ref, idx_o, gate_o, aff_ref):
    E = N_EXPERTS
    m = lg_ref[0]
    for e in range(1, E):
        m = jnp.maximum(m, lg_ref[e])
    z = jnp.zeros_like(m)
    for e in range(E):
        z = z + jnp.exp(lg_ref[e] - m)
    for e in range(E):
        aff_ref[e] = jnp.exp(lg_ref[e] - m) / z

    li = lax.broadcasted_iota(jnp.int32, (LANES, LANES), 0)
    lj = lax.broadcasted_iota(jnp.int32, (LANES, LANES), 1)
    upper = jnp.where(li <= lj, 1.0, 0.0).astype(BF16)
    bi = lax.broadcasted_iota(jnp.int32, (nb, nb), 0)
    bj = lax.broadcasted_iota(jnp.int32, (nb, nb), 1)
    lstrict = jnp.where(bj < bi, 1.0, 0.0).astype(BF16)
    qrow = lax.broadcasted_iota(jnp.int32, (1, cap), 1).astype(F32)
    brow = lax.broadcasted_iota(jnp.int32, (nb, cap), 0).astype(F32)
    irow = lax.broadcasted_iota(jnp.int32, (LANES, cap), 0).astype(F32)

    def prefix(mask):
        cum = _dot(mask.astype(BF16), upper)
        tot = cum[:, LANES - 1:LANES]
        off = _dot(lstrict, jnp.broadcast_to(tot, (nb, LANES)).astype(BF16))
        return cum, off, tot

    def expert(e, carry):
        aff = aff_ref[e]
        bits = pltpu.bitcast(aff, jnp.int32)

        def bisect(_, lohi):
            lo, hi = lohi
            mid = lo + lax.shift_right_arithmetic(hi - lo, 1)
            cnt = jnp.sum(jnp.where(bits >= mid, 1.0, 0.0), keepdims=True)
            ok = cnt >= cap
            return jnp.where(ok, mid, lo), jnp.where(ok, hi, mid)

        lo0 = jnp.zeros((1, 1), jnp.int32)
        hi0 = jnp.full((1, 1), 0x7F800000, jnp.int32)
        thr, _ = lax.fori_loop(0, 31, bisect, (lo0, hi0))
        gt = jnp.where(bits > thr, 1.0, 0.0)
        eq = jnp.where(bits == thr, 1.0, 0.0)
        need = cap - jnp.sum(gt, keepdims=True)
        ecum, eoff, _ = prefix(eq)
        sel = gt + eq * jnp.where(ecum + eoff <= need, 1.0, 0.0)
        cum, off, tot = prefix(sel)
        full = jnp.where(off[:, 0:1] + tot <= qrow, 1.0, 0.0)
        nfull = jnp.sum(full, axis=0, keepdims=True)
        offsel = jnp.sum(full * tot, axis=0, keepdims=True)
        onehot_b = jnp.where(brow == nfull, 1.0, 0.0).astype(BF16)
        cum_t = _dot_tn(cum.astype(BF16), onehot_b)
        within = jnp.sum(jnp.where(cum_t <= qrow - offsel, 1.0, 0.0), axis=0, keepdims=True)
        idx_o[e] = (nfull * LANES + within).astype(jnp.int32)
        a1 = aff.astype(BF16)
        r1 = aff - a1.astype(F32)
        a2 = r1.astype(BF16)
        a3 = (r1 - a2.astype(F32)).astype(BF16)
        aff_t = _dot_tn(a1, onehot_b) + _dot_tn(a2, onehot_b) + _dot_tn(a3, onehot_b)
        gate_o[e] = jnp.sum(jnp.where(irow == within, aff_t, 0.0), axis=0, keepdims=True)
        return carry

    lax.fori_loop(0, E, expert, 0)


def _route(logits_t, cap):
    E, S = logits_t.shape
    nb = S // LANES
    return pl.pallas_call(
        functools.partial(_route_kernel, nb, cap),
        out_shape=[jax.ShapeDtypeStruct((E, 1, cap), jnp.int32), jax.ShapeDtypeStruct((E, 1, cap), F32)],
        scratch_shapes=[pltpu.VMEM((E, nb, LANES), F32)],
        compiler_params=pltpu.CompilerParams(vmem_limit_bytes=VMEM_LIMIT),
        name="moe_route",
    )(logits_t.reshape(E, nb, LANES))


DMA_GROUPS = 4


def _expert_kernel(tq, nsteps, idx_ref, xacc_hbm, xn_hbm, gate_ref, wg_ref, wu_ref, wd_ref, out_hbm,
                   xs_ref, ob_ref, tok_sems, row_sem, sct_sem):
    del xacc_hbm
    n = pl.program_id(0) * pl.num_programs(1) + pl.program_id(1)
    slot = lax.rem(n, 2)

    def hbm_row(ref, tok):
        return ref.at[lax.shift_right_logical(tok, 3), pl.ds(lax.bitwise_and(tok, SUBLANES - 1), 1), :]

    def token_copy(step, buf, i, u):
        tok = idx_ref[step * tq + i * SUBLANES + u]
        return pltpu.make_async_copy(hbm_row(xn_hbm, tok), xs_ref.at[buf, i, pl.ds(u, 1), :], tok_sems.at[buf])

    def row_copy(step, i, u):
        tok = idx_ref[step * tq + i * SUBLANES + u]
        return pltpu.make_async_copy(hbm_row(out_hbm, tok), ob_ref.at[i, pl.ds(u, 1), :], row_sem.at[0])

    def scatter_copy(step, i, u):
        tok = idx_ref[step * tq + i * SUBLANES + u]
        return pltpu.make_async_copy(ob_ref.at[i, pl.ds(u, 1), :], hbm_row(out_hbm, tok), sct_sem.at[0])

    def for_rows(fn):
        def body(i2, c):
            for g in range(DMA_GROUPS):
                for u in range(SUBLANES):
                    fn(i2 * DMA_GROUPS + g, u)
            return c
        lax.fori_loop(0, tq // (SUBLANES * DMA_GROUPS), body, 0)

    @pl.when(n == 0)
    def _():
        for_rows(lambda i, u: token_copy(0, 0, i, u).start())

    @pl.when(n + 1 < nsteps)
    def _():
        for_rows(lambda i, u: token_copy(n + 1, 1 - slot, i, u).start())

    for_rows(lambda i, u: token_copy(n, slot, i, u).wait())
    xs = xs_ref[slot].reshape(tq, D_MODEL).astype(BF16)
    hg = _dot(xs, wg_ref[...])
    hu = _dot(xs, wu_ref[...])
    hdn = ((hg * _sigmoid(hg)) * hu).astype(BF16)

    @pl.when(n > 0)
    def _():
        for_rows(lambda i, u: scatter_copy(n - 1, i, u).wait())

    for_rows(lambda i, u: row_copy(n, i, u).start())
    ye = _dot(hdn, wd_ref[...])
    for_rows(lambda i, u: row_copy(n, i, u).wait())
    upd = ob_ref[...].reshape(tq, D_MODEL) + ye * gate_ref[...]
    ob_ref[...] = upd.reshape(tq // SUBLANES, SUBLANES, D_MODEL)
    for_rows(lambda i, u: scatter_copy(n, i, u).start())

    @pl.when(n == nsteps - 1)
    def _():
        for_rows(lambda i, u: scatter_copy(n, i, u).wait())


def _experts(x, xn, idx, gate, l, p):
    S = x.shape[0]
    E, cap = idx.shape
    tq = min(512, cap)
    nt = cap // tq
    grid_spec = pltpu.PrefetchScalarGridSpec(
        num_scalar_prefetch=1,
        grid=(E, nt),
        in_specs=[
            pl.BlockSpec(memory_space=pl.ANY),
            pl.BlockSpec(memory_space=pl.ANY),
            pl.BlockSpec((None, tq, 1), lambda e, t, idx: (e, t, 0)),
            pl.BlockSpec((None, D_MODEL, EXPERT_FF), lambda e, t, idx: (e, 0, 0)),
            pl.BlockSpec((None, D_MODEL, EXPERT_FF), lambda e, t, idx: (e, 0, 0)),
            pl.BlockSpec((None, EXPERT_FF, D_MODEL), lambda e, t, idx: (e, 0, 0)),
        ],
        out_specs=pl.BlockSpec(memory_space=pl.ANY),
        scratch_shapes=[pltpu.VMEM((2, tq // SUBLANES, SUBLANES, D_MODEL), F32),
                        pltpu.VMEM((tq // SUBLANES, SUBLANES, D_MODEL), F32),
                        pltpu.SemaphoreType.DMA((2,)), pltpu.SemaphoreType.DMA((1,)),
                        pltpu.SemaphoreType.DMA((1,))],
    )
    grouped = (S // SUBLANES, SUBLANES, D_MODEL)
    out = pl.pallas_call(
        functools.partial(_expert_kernel, tq, E * nt),
        grid_spec=grid_spec,
        out_shape=jax.ShapeDtypeStruct(grouped, F32),
        input_output_aliases={1: 0},
        compiler_params=_cparams(("arbitrary", "arbitrary")),
        name="moe_experts",
    )(idx.reshape(E * cap), x.reshape(grouped), xn.reshape(grouped), gate.reshape(E, cap, 1),
      p['w_gate'][l].astype(BF16), p['w_up'][l].astype(BF16), p['w_down'][l].astype(BF16))
    return out.reshape(S, D_MODEL)


def _trunk(x, positions, p):
    S = x.shape[0]
    cap = CAPACITY_FACTOR * S // N_EXPERTS
    depth = p['w_in'].shape[0]
    v_first = None
    for l in range(depth):
        xn = _rmsnorm(x, p['norm_mix'][l], BF16)
        h_rwkv, h_lru, h_mla, w_gates = _in_projection(xn, p['w_in'][l], p['rwkv_vres_down'][l - 1] if l > 0 else None)
        r, v, kk, lw, kd, a, g, bonus = _rwkv_prep(h_rwkv, h_mla, l, p, v_first)
        if l == 0:
            v_first = v
        yf, yb = _rwkv_scan(r, v, kk, lw, kd, a)
        ya = _rwkv_post(yf, yb, bonus, g, p['rwkv_lnx_w'][l], p['rwkv_lnx_b'][l])
        hs0 = _lru_scan(h_lru, l, p, 0)
        hs1 = _lru_scan(h_lru, l, p, 1)
        q, k, vv = _mla_prep(h_mla, positions, l, p)
        yc = _flash_attention(q, k, vv)
        merged = _merge(xn, ya, hs0, hs1, h_lru, yc, w_gates, l, p)
        x = _matmul(merged, p['w_out'][l].astype(BF16), D_MODEL, res=x, name="outproj")
        xn2, logits_t = _router(x, p['norm_ffn'][l], p['w_router'][l])
        idx, gate = _route(logits_t, cap)
        x = _experts(x, xn2, idx.reshape(N_EXPERTS, cap), gate.reshape(N_EXPERTS, cap), l, p)
    return _rmsnorm(x, p['norm_final'], F32)


def kernel(x, positions, norm_mix, w_in, rwkv_shift_mu, rwkv_w0, rwkv_w2, rwkv_a0, rwkv_a2, rwkv_g2, rwkv_k_k,
           rwkv_k_a, rwkv_r_k, rwkv_lnx_w, rwkv_lnx_b, rwkv_vres_down, rwkv_vres_up, rwkv_vres_b, lru_conv_w,
           lru_conv_b, lru_wa, lru_ba, lru_wx, lru_bx, lru_lambda, mla_q_norm, mla_kv_norm, mla_w_uq, mla_w_uk,
           mla_w_uv, wo_rwkv, wo_lru, wo_mla, w_out, norm_ffn, w_router, w_gate, w_up, w_down, norm_final):
    p = dict(norm_mix=norm_mix, w_in=w_in, rwkv_shift_mu=rwkv_shift_mu, rwkv_w0=rwkv_w0, rwkv_w2=rwkv_w2,
             rwkv_a0=rwkv_a0, rwkv_a2=rwkv_a2, rwkv_g2=rwkv_g2, rwkv_k_k=rwkv_k_k, rwkv_k_a=rwkv_k_a,
             rwkv_r_k=rwkv_r_k, rwkv_lnx_w=rwkv_lnx_w, rwkv_lnx_b=rwkv_lnx_b, rwkv_vres_down=rwkv_vres_down,
             rwkv_vres_up=rwkv_vres_up, rwkv_vres_b=rwkv_vres_b, lru_conv_w=lru_conv_w, lru_conv_b=lru_conv_b,
             lru_wa=lru_wa, lru_ba=lru_ba, lru_wx=lru_wx, lru_bx=lru_bx, lru_lambda=lru_lambda,
             mla_q_norm=mla_q_norm, mla_kv_norm=mla_kv_norm, mla_w_uq=mla_w_uq, mla_w_uk=mla_w_uk,
             mla_w_uv=mla_w_uv, wo_rwkv=wo_rwkv, wo_lru=wo_lru, wo_mla=wo_mla, w_out=w_out, norm_ffn=norm_ffn,
             w_router=w_router, w_gate=w_gate, w_up=w_up, w_down=w_down, norm_final=norm_final)
    B, S, D = x.shape
    outs = [_trunk(x[b], positions[b], p) for b in range(B)]
    return jnp.stack(outs, axis=0)
```

```python
import functools

import numpy as np
import jax
import jax.numpy as jnp
from jax import lax
from jax.experimental import pallas as pl
from jax.experimental.pallas import tpu as pltpu

F32 = jnp.float32
BF16 = jnp.bfloat16

D_MODEL = 2048
RWKV_HEADS = 12
RWKV_HEAD = 64
RWKV_W = RWKV_HEADS * RWKV_HEAD
DECAY_LORA = 64
ICLR_LORA = 64
VRES_LORA = 32
GATE_LORA = 128
GN_EPS = 64e-5
LRU_BLOCKS = 6
LRU_BLOCK = 128
LRU_W = LRU_BLOCKS * LRU_BLOCK
CONV_W = 4
LRU_C = 8.0
MLA_HEADS = 4
QK_NOPE = 128
QK_ROPE = 64
V_HEAD = 128
Q_LORA = 384
KV_LORA = 128
MLA_W = MLA_HEADS * V_HEAD
ROPE_THETA = 10000.0
N_BRANCH = 3
N_EXPERTS = 16
EXPERT_FF = 1024
CAPACITY_FACTOR = 2
NORM_EPS = 1e-6
RWKV_COLS = 3 * RWKV_W + 2 * DECAY_LORA + 2 * ICLR_LORA + GATE_LORA

LANES = 128
SUBLANES = 8
VMEM_LIMIT = 56 * 1024 * 1024

SEG_LRU = RWKV_COLS
SEG_MLA = SEG_LRU + 2 * LRU_W
SEG_GATES = SEG_MLA + Q_LORA + KV_LORA + QK_ROPE
MLA_SEG_W = Q_LORA + KV_LORA + 2 * LANES
MLA_COL_KR = Q_LORA + KV_LORA
MLA_COL_VRES = MLA_COL_KR + LANES

CHUNK = 64
PAIR = 2 * RWKV_HEAD
NPAIR = RWKV_W // PAIR


def _cparams(sem):
    return pltpu.CompilerParams(dimension_semantics=sem, vmem_limit_bytes=VMEM_LIMIT)


def _dot(a, b):
    return jnp.dot(a, b, preferred_element_type=F32)


def _dot_nt(a, b):
    return lax.dot_general(a, b, (((1,), (1,)), ((), ())), preferred_element_type=F32)


def _dot_tn(a, b):
    return lax.dot_general(a, b, (((0,), (0,)), ((), ())), preferred_element_type=F32)


def _split(x):
    hi = x.astype(BF16)
    lo = (x - hi.astype(F32)).astype(BF16)
    return hi, lo


def _dot_exactb(x, b):
    hi, lo = _split(x)
    return _dot(hi, b) + _dot(lo, b)


def _dot3(a, b):
    ah, al = _split(a)
    bh, bl = _split(b)
    return _dot(ah, bh) + (_dot(ah, bl) + _dot(al, bh))


def _dot3_nt(a, b):
    ah, al = _split(a)
    bh, bl = _split(b)
    return _dot_nt(ah, bh) + (_dot_nt(ah, bl) + _dot_nt(al, bh))


def _dot3_tn(a, b):
    ah, al = _split(a)
    bh, bl = _split(b)
    return _dot_tn(ah, bh) + (_dot_tn(ah, bl) + _dot_tn(al, bh))


def _sigmoid(x):
    return 1.0 / (1.0 + jnp.exp(-x))


def _softplus(x):
    return jnp.maximum(x, 0.0) + jnp.log(1.0 + jnp.exp(-jnp.abs(x)))


def _rmsnorm_kernel(x_ref, g_ref, o_ref):
    x = x_ref[...]
    ms = jnp.mean(x * x, axis=-1, keepdims=True)
    o_ref[...] = (x * lax.rsqrt(ms + NORM_EPS) * g_ref[...]).astype(o_ref.dtype)


def _rmsnorm(x, gain, dtype):
    S, D = x.shape
    tm = min(512, S)
    return pl.pallas_call(
        _rmsnorm_kernel,
        grid=(S // tm,),
        in_specs=[pl.BlockSpec((tm, D), lambda i: (i, 0)), pl.BlockSpec((1, D), lambda i: (0, 0))],
        out_specs=pl.BlockSpec((tm, D), lambda i: (i, 0)),
        out_shape=jax.ShapeDtypeStruct((S, D), dtype),
        compiler_params=_cparams(("parallel",)),
        name="rmsnorm",
    )(x, gain.reshape(1, D))


def _matmul_kernel(has_res, *refs):
    if has_res:
        a_ref, w_ref, res_ref, o_ref = refs
        o_ref[...] = res_ref[...] + _dot(a_ref[...], w_ref[...])
    else:
        a_ref, w_ref, o_ref = refs
        o_ref[...] = _dot(a_ref[...], w_ref[...])


def _matmul(a, w, tn, res=None, name="matmul"):
    M, K = a.shape
    N = w.shape[1]
    tm = min(512, M)
    in_specs = [pl.BlockSpec((tm, K), lambda j, i: (i, 0)), pl.BlockSpec((K, tn), lambda j, i: (0, j))]
    args = [a, w]
    if res is not None:
        in_specs.append(pl.BlockSpec((tm, tn), lambda j, i: (i, j)))
        args.append(res)
    return pl.pallas_call(
        functools.partial(_matmul_kernel, res is not None),
        grid=(N // tn, M // tm),
        in_specs=in_specs,
        out_specs=pl.BlockSpec((tm, tn), lambda j, i: (i, j)),
        out_shape=jax.ShapeDtypeStruct((M, N), F32),
        compiler_params=_cparams(("parallel", "parallel")),
        name=name,
    )(*args)


def _in_projection(xn, w_in, vres_down):
    w = w_in.astype(BF16)
    w_mla = jnp.zeros((D_MODEL, MLA_SEG_W), BF16)
    w_mla = w_mla.at[:, :MLA_COL_KR + QK_ROPE].set(w[:, SEG_MLA:SEG_GATES])
    if vres_down is not None:
        w_mla = w_mla.at[:, MLA_COL_VRES:MLA_COL_VRES + VRES_LORA].set(vres_down.astype(BF16))
    h_rwkv = _matmul(xn, w[:, :SEG_LRU], RWKV_COLS, name="inproj_rwkv")
    h_lru = _matmul(xn, w[:, SEG_LRU:SEG_MLA], 2 * LRU_W, name="inproj_lru")
    h_mla = _matmul(xn, w_mla, MLA_SEG_W, name="inproj_mla")
    return h_rwkv, h_lru, h_mla, w[:, SEG_GATES:]


def _rwkv_prep_kernel(has_vres, tm, nblk, *refs):
    if has_vres:
        (h_ref, hp_ref, hn_ref, lora_ref, vfirst_ref, vup_ref, vb_ref, *rest) = refs
    else:
        (h_ref, hp_ref, hn_ref, *rest) = refs
    (mu_ref, w0_ref, w2_ref, a0_ref, a2_ref, g2_ref, kkw_ref, ka_ref, rk_ref, bd_ref,
     r_o, v_o, kk_o, lw_o, kd_o, a_o, g_o, bonus_o) = rest
    i = pl.program_id(0)
    p = h_ref[...]
    rows = lax.broadcasted_iota(jnp.int32, (tm, 1), 0)
    prev_row = jnp.where(i == 0, 0.0, hp_ref[SUBLANES - 1:SUBLANES, :])
    next_row = jnp.where(i == nblk - 1, 0.0, hn_ref[0:1, :])
    prev = jnp.where(rows == 0, prev_row, pltpu.roll(p, 1, 0))
    nxt = jnp.where(rows == tm - 1, next_row, pltpu.roll(p, tm - 1, 0))
    hs = p + mu_ref[0:1, :] * (prev - p) + mu_ref[1:2, :] * (nxt - p)

    W = RWKV_W
    r = hs[:, 0:W]
    k = hs[:, W:2 * W]
    v = hs[:, 2 * W:3 * W]
    wd = hs[:, 3 * W:3 * W + LANES]
    ad = hs[:, 3 * W + LANES:3 * W + 2 * LANES]
    gd = hs[:, 3 * W + 2 * LANES:3 * W + 3 * LANES]
    def put(ref, val, *lead):
        for pr in range(NPAIR):
            ref[(*lead, pr)] = val[:, pr * PAIR:(pr + 1) * PAIR]

    if has_vres:
        vg = _sigmoid(vb_ref[...] + _dot(lora_ref[...].astype(BF16), vup_ref[...]))
        v_first = jnp.concatenate([vfirst_ref[pr] for pr in range(NPAIR)], axis=1)
        v = v + (v_first - v) * vg
    bd = bd_ref[...]
    kk = k * kkw_ref[...]
    n2 = _dot_exactb(kk * kk, bd)
    kk = kk / jnp.maximum(jnp.sqrt(n2), 1e-12)
    tw = jnp.tanh(wd).astype(BF16)
    adb = ad.astype(BF16)
    ksum = None
    for d in range(2):
        wl = w0_ref[d:d + 1, :] + _dot(tw, w2_ref[d])
        w_log = -_softplus(-wl) - 0.5
        put(lw_o, -jnp.exp(w_log), d)
        a = _sigmoid(a0_ref[d:d + 1, :] + _dot(adb, a2_ref[d]))
        kd = k * (1.0 + (a - 1.0) * ka_ref[...])
        put(a_o, a, d)
        put(kd_o, kd, d)
        ksum = kd if ksum is None else ksum + kd
    put(r_o, r)
    put(v_o, v)
    put(kk_o, kk)
    g_o[...] = _dot(_sigmoid(gd).astype(BF16), g2_ref[...])
    bonus_o[...] = _dot_exactb(r * ksum * rk_ref[...], bd) * v


def _head_blockdiag():
    hid = np.arange(RWKV_W) // RWKV_HEAD
    return jnp.asarray((hid[:, None] == hid[None, :]).astype(np.float32), dtype=BF16)


def _rwkv_prep(h, h_mla, l, p, v_first):
    S = h.shape[0]
    tm = min(256, S)
    nblk = S // tm
    has_vres = l > 0
    W = RWKV_W
    hb = tm // SUBLANES
    nrow8 = S // SUBLANES

    def zpad(w, rows_before):
        z = jnp.zeros((LANES, W), F32)
        return z.at[rows_before:rows_before + w.shape[0]].set(w)

    w2 = jnp.stack([zpad(p['rwkv_w2'][l, 0], 0), zpad(p['rwkv_w2'][l, 1], DECAY_LORA)]).astype(BF16)
    a2 = jnp.stack([zpad(p['rwkv_a2'][l, 0], 0), zpad(p['rwkv_a2'][l, 1], ICLR_LORA)]).astype(BF16)
    const2 = lambda shape: pl.BlockSpec(shape, lambda i: (0,) * len(shape))
    in_specs = [
        pl.BlockSpec((tm, RWKV_COLS), lambda i: (i, 0)),
        pl.BlockSpec((SUBLANES, RWKV_COLS), lambda i: (jnp.maximum(i * hb - 1, 0), 0)),
        pl.BlockSpec((SUBLANES, RWKV_COLS), lambda i: (jnp.minimum((i + 1) * hb, nrow8 - 1), 0)),
    ]
    args = [h, h, h]
    if has_vres:
        in_specs += [
            pl.BlockSpec((tm, LANES), lambda i: (i, MLA_COL_VRES // LANES)),
            pl.BlockSpec((NPAIR, tm, PAIR), lambda i: (0, i, 0)),
            const2((LANES, W)),
            const2((1, W)),
        ]
        vup = jnp.zeros((LANES, W), F32).at[:VRES_LORA].set(p['rwkv_vres_up'][l - 1]).astype(BF16)
        args += [h_mla, v_first, vup, p['rwkv_vres_b'][l - 1].reshape(1, W)]
    in_specs += [
        const2((2, RWKV_COLS)), const2((2, W)), const2((2, LANES, W)), const2((2, W)), const2((2, LANES, W)),
        const2((LANES, W)), const2((1, W)), const2((1, W)), const2((1, W)), const2((W, W)),
    ]
    args += [
        p['rwkv_shift_mu'][l], p['rwkv_w0'][l], w2, p['rwkv_a0'][l], a2,
        p['rwkv_g2'][l].astype(BF16), p['rwkv_k_k'][l].reshape(1, W), p['rwkv_k_a'][l].reshape(1, W),
        p['rwkv_r_k'][l].reshape(1, W), _head_blockdiag(),
    ]
    row = pl.BlockSpec((tm, W), lambda i: (i, 0))
    prow = pl.BlockSpec((NPAIR, tm, PAIR), lambda i: (0, i, 0))
    prow2 = pl.BlockSpec((2, NPAIR, tm, PAIR), lambda i: (0, 0, i, 0))
    sds = jax.ShapeDtypeStruct((S, W), F32)
    psds = jax.ShapeDtypeStruct((NPAIR, S, PAIR), F32)
    psds2 = jax.ShapeDtypeStruct((2, NPAIR, S, PAIR), F32)
    return pl.pallas_call(
        functools.partial(_rwkv_prep_kernel, has_vres, tm, nblk),
        grid=(nblk,),
        in_specs=in_specs,
        out_specs=[prow, prow, prow, prow2, prow2, prow2, row, row],
        out_shape=[psds, psds, psds, psds2, psds2, psds2, sds, sds],
        compiler_params=_cparams(("parallel",)),
        name="rwkv_prep",
    )(*args)


N_MASK = 8


def _scan_masks():
    n = PAIR
    out = np.zeros((2, N_MASK, n, n), np.float32)
    r = np.arange(n)[:, None]
    c = np.arange(n)[None, :]
    same = (r // CHUNK) == (c // CHUNK)
    for d in range(2):
        before = (c < r) if d == 0 else (c > r)
        strict = same & before
        out[d, 0] = strict
        out[d, 1] = same & (before | (c == r))
        out[d, 2] = strict & ((r // 8) == (c // 8))
        for j, b in enumerate((8, 16, 32)):
            out[d, 3 + j] = strict & ((r // (2 * b)) == (c // (2 * b))) & ((r // b) != (c // b))
        out[d, 6] = (r == c)
        out[d, 7] = (before | (c == r)) & (r < CHUNK) & (c < CHUNK)
    return jnp.asarray(out)


def _mm(a, b):
    return _dot(a.astype(BF16), b.astype(BF16))


def _mm_nt(a, b):
    return _dot_nt(a.astype(BF16), b.astype(BF16))


def _mm_tn(a, b):
    return _dot_tn(a.astype(BF16), b.astype(BF16))


def _rwkv_scan_kernel(nsub, rf, vf, kkf, lwf, kf, af, rb, vb, kkb, lwb, kb_, ab, m_ref, yf, yb, s_ref):
    T = CHUNK
    dirs = ((rf, vf, kkf, lwf, kf, af, yf), (rb, vb, kkb, lwb, kb_, ab, yb))

    @pl.when(pl.program_id(1) == 0)
    def _():
        s_ref[...] = jnp.zeros_like(s_ref)

    lane = lax.broadcasted_iota(jnp.int32, (1, PAIR), 1)
    m0 = jnp.where(lane < RWKV_HEAD, 1.0, 0.0)
    m1 = 1.0 - m0

    def stack(x):
        return jnp.concatenate([x * m0, x * m1], axis=0)

    def local(d, j):
        r_ref, v_ref, kk_ref, lw_ref, k_ref, a_ref, _ = dirs[d]
        sl = slice(j * T, (j + 1) * T)
        strict, incl, eye = m_ref[d, 0], m_ref[d, 1], m_ref[d, 6]
        lw = lw_ref[sl, :]
        kk = kk_ref[sl, :]
        beta = a_ref[sl, :] * kk
        kd = k_ref[sl, :]
        cs = m_ref[d, 7][0:T, 0:T].astype(BF16)
        lh, ll = _split(lw)
        c = _dot(cs, lh) + _dot(cs, ll)
        c_tot = c[0:1, :] if d == 1 else c[T - 1:T, :]
        e_inv = jnp.exp(-c)
        e_fin = jnp.exp(c_tot - c)
        kt = stack(kk * jnp.exp(c - lw))
        rt = stack(r_ref[sl, :] * jnp.exp(c))
        bb = stack(beta * e_inv)
        kb = stack(kd * e_inv)
        bh = stack(beta * e_fin)
        kh = stack(kd * e_fin)
        vs = stack(v_ref[sl, :])

        yield
        p1 = _mm_nt(jnp.concatenate([kt, rt], axis=0), jnp.concatenate([bb, kb], axis=0))
        A = p1[0:PAIR, 0:PAIR] * strict
        B = p1[0:PAIR, PAIR:] * strict
        C = p1[PAIR:, 0:PAIR] * incl
        E = p1[PAIR:, PAIR:] * incl
        yield

        d8 = A * m_ref[d, 2]
        d2 = _mm(d8, d8)
        bv = _mm(B, vs)
        yield
        d4 = _mm(d2, d2)
        x = _mm(eye - d8, eye + d2)
        yield
        x = _mm(x, eye + d4)
        yield
        for lvl in range(3):
            aoff = A * m_ref[d, 3 + lvl]
            ax = _mm(aoff, x)
            yield
            x = x - _mm(x, ax)
            yield

        mu = -_mm(x, jnp.concatenate([kt, bv], axis=1))
        ev = _mm(E, vs)
        hv = _mm_tn(kh, vs)
        yield
        cm = _mm(C, mu)
        gh = _mm_tn(bh, mu)
        yield
        rm = rt + cm[:, 0:PAIR]
        y0 = cm[:, PAIR:] + ev
        p_col = jnp.sum(eye * jnp.exp(c_tot), axis=1, keepdims=True)
        Gp = gh[:, 0:PAIR].astype(BF16)
        H = gh[:, PAIR:] + hv
        return rm.astype(BF16), y0, p_col, Gp, H

    keys = [(d, j) for d in range(2) for j in range(nsub)]
    gens = [local(d, j) for d, j in keys]
    loc = {}
    while len(loc) < len(keys):
        for key, gen in zip(keys, gens):
            try:
                next(gen)
            except StopIteration as done:
                loc[key] = done.value
    s = [s_ref[0], s_ref[1]]
    for step in range(nsub):
        for d in range(2):
            j = step if d == 0 else nsub - 1 - step
            rm, y0, p_col, Gp, H = loc[d, j]
            sb = s[d].astype(BF16)
            yst = _dot(rm, sb) + y0
            dirs[d][6][j * T:(j + 1) * T, :] = yst[0:T, :] + yst[T:, :]
            s[d] = p_col * s[d] + (_dot(Gp, sb) + H)
    s_ref[0] = s[0]
    s_ref[1] = s[1]


def _rwkv_scan(r, v, kk, lw, kd, a):
    S = r.shape[1]
    tb = min(1024, S)
    nsub = tb // CHUNK
    nblk = S // tb
    fwd = pl.BlockSpec((None, tb, PAIR), lambda pr, c: (pr, c, 0))
    bwd = pl.BlockSpec((None, tb, PAIR), lambda pr, c: (pr, nblk - 1 - c, 0))
    fwd2 = pl.BlockSpec((None, None, tb, PAIR), lambda pr, c: (0, pr, c, 0))
    bwd2 = pl.BlockSpec((None, None, tb, PAIR), lambda pr, c: (1, pr, nblk - 1 - c, 0))
    sds = jax.ShapeDtypeStruct((NPAIR, S, PAIR), F32)
    return pl.pallas_call(
        functools.partial(_rwkv_scan_kernel, nsub),
        grid=(NPAIR, nblk),
        in_specs=[fwd, fwd, fwd, fwd2, fwd2, fwd2, bwd, bwd, bwd, bwd2, bwd2, bwd2,
                  pl.BlockSpec((2, N_MASK, PAIR, PAIR), lambda pr, c: (0, 0, 0, 0))],
        out_specs=[fwd, bwd],
        out_shape=[sds, sds],
        scratch_shapes=[pltpu.VMEM((2, PAIR, PAIR), F32)],
        compiler_params=_cparams(("parallel", "arbitrary")),
        name="rwkv_scan",
    )(r, v, kk, lw, kd, a, r, v, kk, lw, kd, a, _scan_masks())


def _rwkv_post_kernel(yf_ref, yb_ref, bonus_ref, g_ref, w_ref, b_ref, bd_ref, o_ref):
    y = jnp.concatenate([yf_ref[pr] + yb_ref[pr] for pr in range(NPAIR)], axis=1)
    bd = bd_ref[...]
    mu = _dot_exactb(y, bd) * (1.0 / RWKV_HEAD)
    yc = y - mu
    var = _dot_exactb(yc * yc, bd) * (1.0 / RWKV_HEAD)
    yn = yc * lax.rsqrt(var + GN_EPS) * w_ref[...] + b_ref[...]
    o_ref[...] = (yn + bonus_ref[...]) * g_ref[...]


def _rwkv_post(yf, yb, bonus, g, lnx_w, lnx_b):
    S = bonus.shape[0]
    tm = min(512, S)
    W = RWKV_W
    row = pl.BlockSpec((tm, W), lambda i: (i, 0))
    prow = pl.BlockSpec((NPAIR, tm, PAIR), lambda i: (0, i, 0))
    return pl.pallas_call(
        _rwkv_post_kernel,
        grid=(S // tm,),
        in_specs=[prow, prow, row, row,
                  pl.BlockSpec((1, W), lambda i: (0, 0)), pl.BlockSpec((1, W), lambda i: (0, 0)),
                  pl.BlockSpec((W, W), lambda i: (0, 0))],
        out_specs=row,
        out_shape=jax.ShapeDtypeStruct((S, W), F32),
        compiler_params=_cparams(("parallel",)),
        name="rwkv_post",
    )(yf, yb, bonus, g, lnx_w.reshape(1, W), lnx_b.reshape(1, W), _head_blockdiag())


def _lru_kernel(rev, tm, nblk, x_ref, xp_ref, xn_ref, cw_ref, cb_ref, wa_ref, ba_ref, wx_ref, bx_ref, lam_ref,
                o_ref, carry_ref):
    c = pl.program_id(0)
    ib = nblk - 1 - c if rev else c

    @pl.when(c == 0)
    def _():
        carry_ref[...] = jnp.zeros_like(carry_ref)

    x = x_ref[...]
    rows = lax.broadcasted_iota(jnp.int32, (tm, 1), 0)
    prev_row = jnp.where(ib == 0, 0.0, xp_ref[SUBLANES - 1:SUBLANES, :])
    nx0 = jnp.where(ib == nblk - 1, 0.0, xn_ref[0:1, :])
    nx1 = jnp.where(ib == nblk - 1, 0.0, xn_ref[1:2, :])
    xm1 = jnp.where(rows == 0, prev_row, pltpu.roll(x, 1, 0))
    xp1 = jnp.where(rows == tm - 1, nx0, pltpu.roll(x, tm - 1, 0))
    xp2 = jnp.where(rows == tm - 2, nx0, jnp.where(rows == tm - 1, nx1, pltpu.roll(x, tm - 2, 0)))
    xc = (cw_ref[0:1, :] * xm1 + cw_ref[1:2, :] * x + cw_ref[2:3, :] * xp1 + cw_ref[3:4, :] * xp2) + cb_ref[...]
    xcb = xc.astype(BF16)
    ra, ia = [], []
    for g in range(LRU_BLOCKS):
        blk = xcb[:, g * LRU_BLOCK:(g + 1) * LRU_BLOCK]
        ra.append(_dot(blk, wa_ref[g]))
        ia.append(_dot(blk, wx_ref[g]))
    rg = _sigmoid(jnp.concatenate(ra, axis=1) + ba_ref[...])
    ig = _sigmoid(jnp.concatenate(ia, axis=1) + bx_ref[...])
    log_a = -LRU_C * rg * _softplus(-lam_ref[...])
    a = jnp.exp(log_a)
    b = jnp.sqrt(1.0 - jnp.exp(2.0 * log_a)) * (ig * xc)
    k = 1
    while k < tm:
        if rev:
            ap, bp = pltpu.roll(a, tm - k, 0), pltpu.roll(b, tm - k, 0)
            valid = rows < tm - k
        else:
            ap, bp = pltpu.roll(a, k, 0), pltpu.roll(b, k, 0)
            valid = rows >= k
        b = jnp.where(valid, a * bp + b, b)
        a = jnp.where(valid, a * ap, a)
        k *= 2
    h = a * carry_ref[0:1, :] + b
    o_ref[...] = h
    last = h[0:1, :] if rev else h[tm - 1:tm, :]
    carry_ref[...] = jnp.broadcast_to(last, carry_ref.shape)


def _lru_scan(h_lru, l, p, d):
    S = h_lru.shape[0]
    tm = min(512, S)
    nblk = S // tm
    W = LRU_W
    rev = d == 1
    hb = tm // SUBLANES
    nrow8 = S // SUBLANES
    blk = (lambda c: nblk - 1 - c) if rev else (lambda c: c)
    const = lambda shape: pl.BlockSpec(shape, lambda c: (0,) * len(shape))
    return pl.pallas_call(
        functools.partial(_lru_kernel, rev, tm, nblk),
        grid=(nblk,),
        in_specs=[
            pl.BlockSpec((tm, W), lambda c: (blk(c), 0)),
            pl.BlockSpec((SUBLANES, W), lambda c: (jnp.maximum(blk(c) * hb - 1, 0), 0)),
            pl.BlockSpec((SUBLANES, W), lambda c: (jnp.minimum((blk(c) + 1) * hb, nrow8 - 1), 0)),
            const((CONV_W, W)), const((1, W)),
            const((LRU_BLOCKS, LRU_BLOCK, LRU_BLOCK)), const((1, W)),
            const((LRU_BLOCKS, LRU_BLOCK, LRU_BLOCK)), const((1, W)), const((1, W)),
        ],
        out_specs=pl.BlockSpec((tm, W), lambda c: (blk(c), 0)),
        out_shape=jax.ShapeDtypeStruct((S, W), F32),
        scratch_shapes=[pltpu.VMEM((SUBLANES, W), F32)],
        compiler_params=_cparams(("arbitrary",)),
        name="lru_scan_rev" if rev else "lru_scan_fwd",
    )(h_lru, h_lru, h_lru, p['lru_conv_w'][l], p['lru_conv_b'][l].reshape(1, W),
      p['lru_wa'][l, d].astype(BF16), p['lru_ba'][l, d].reshape(1, W),
      p['lru_wx'][l, d].astype(BF16), p['lru_bx'][l, d].reshape(1, W), p['lru_lambda'][l, d].reshape(1, W))


MLA_QK = 2 * LANES
MLA_Q_SCALE = float((QK_NOPE + QK_ROPE) ** -0.5 * np.log2(np.e))


def _mla_prep_kernel(cq_ref, ckv_ref, kr_ref, pos_ref, invf_ref, qn_ref, kvn_ref, wq_ref, wk_ref, wv_ref,
                     q_o, k_o, v_o):
    def norm(x, g):
        ms = jnp.mean(x * x, axis=-1, keepdims=True)
        return (x * lax.rsqrt(ms + NORM_EPS) * g).astype(BF16)

    q = _dot(norm(cq_ref[...], qn_ref[...]), wq_ref[...]) * MLA_Q_SCALE
    ckv = norm(ckv_ref[...], kvn_ref[...])
    kn = _dot(ckv, wk_ref[...])
    v = _dot(ckv, wv_ref[...])
    ang = pos_ref[...].astype(F32) * invf_ref[...]
    lane = lax.broadcasted_iota(jnp.int32, (1, LANES), 1)
    half = QK_ROPE // 2
    cos = jnp.cos(ang)
    sin = jnp.sin(ang)
    s_lo = jnp.where(lane < half, -sin, 0.0)
    s_hi = jnp.where((lane >= half) & (lane < QK_ROPE), sin, 0.0)

    def rope(t):
        return t * cos + pltpu.roll(t, LANES - half, 1) * s_lo + pltpu.roll(t, half, 1) * s_hi

    kpe = rope(kr_ref[...]).astype(BF16)
    for hh in range(MLA_HEADS):
        q_o[hh, :, 0:LANES] = q[:, hh * MLA_QK:hh * MLA_QK + LANES].astype(BF16)
        q_o[hh, :, LANES:] = rope(q[:, hh * MLA_QK + LANES:(hh + 1) * MLA_QK]).astype(BF16)
        k_o[hh, :, 0:LANES] = kn[:, hh * QK_NOPE:(hh + 1) * QK_NOPE].astype(BF16)
        k_o[hh, :, LANES:] = kpe
        v_o[hh, :, 0:V_HEAD] = v[:, hh * V_HEAD:(hh + 1) * V_HEAD].astype(BF16)
        v_o[hh, :, V_HEAD:] = jnp.ones((v.shape[0], V_HEAD), BF16)


def _mla_prep(h_mla, positions, l, p):
    S = h_mla.shape[0]
    tm = min(512, S)
    half = QK_ROPE // 2
    inv = ROPE_THETA ** (-jnp.arange(0, QK_ROPE, 2, dtype=F32) / QK_ROPE)
    invf = jnp.zeros((1, LANES), F32).at[0, :half].set(inv).at[0, half:QK_ROPE].set(inv)
    wq = p['mla_w_uq'][l].reshape(Q_LORA, MLA_HEADS, QK_NOPE + QK_ROPE)
    wq = jnp.concatenate([wq, jnp.zeros((Q_LORA, MLA_HEADS, MLA_QK - QK_NOPE - QK_ROPE), F32)], axis=-1)
    wq = wq.reshape(Q_LORA, MLA_HEADS * MLA_QK).astype(BF16)
    const = lambda shape: pl.BlockSpec(shape, lambda i: (0,) * len(shape))
    hq = jax.ShapeDtypeStruct((MLA_HEADS, S, MLA_QK), BF16)
    return pl.pallas_call(
        _mla_prep_kernel,
        grid=(S // tm,),
        in_specs=[
            pl.BlockSpec((tm, Q_LORA), lambda i: (i, 0)),
            pl.BlockSpec((tm, KV_LORA), lambda i: (i, Q_LORA // KV_LORA)),
            pl.BlockSpec((tm, LANES), lambda i: (i, MLA_COL_KR // LANES)),
            pl.BlockSpec((tm, 1), lambda i: (i, 0)),
            const((1, LANES)), const((1, Q_LORA)), const((1, KV_LORA)),
            const((Q_LORA, MLA_HEADS * MLA_QK)), const((KV_LORA, MLA_HEADS * QK_NOPE)),
            const((KV_LORA, MLA_HEADS * V_HEAD)),
        ],
        out_specs=[pl.BlockSpec((MLA_HEADS, tm, MLA_QK), lambda i: (0, i, 0)),
                   pl.BlockSpec((MLA_HEADS, tm, MLA_QK), lambda i: (0, i, 0)),
                   pl.BlockSpec((MLA_HEADS, tm, 2 * V_HEAD), lambda i: (0, i, 0))],
        out_shape=[hq, hq, jax.ShapeDtypeStruct((MLA_HEADS, S, 2 * V_HEAD), BF16)],
        compiler_params=_cparams(("parallel",)),
        name="mla_prep",
    )(h_mla, h_mla, h_mla, positions.reshape(S, 1), invf, p['mla_q_norm'][l].reshape(1, Q_LORA),
      p['mla_kv_norm'][l].reshape(1, KV_LORA), wq, p['mla_w_uk'][l].astype(BF16), p['mla_w_uv'][l].astype(BF16))


FLASH_ROWS = 4


def _flash_kernel(nkv, tq, q_ref, k_ref, v_ref, o_ref, m_ref, acc_ref):
    j = pl.program_id(2)

    @pl.when(j == 0)
    def _():
        m_ref[...] = jnp.full_like(m_ref, -jnp.inf)
        acc_ref[...] = jnp.zeros_like(acc_ref)

    tr = tq // FLASH_ROWS
    k = k_ref[...]
    v = v_ref[...]
    scores = [_dot_nt(q_ref[g * tr:(g + 1) * tr, :], k) for g in range(FLASH_ROWS)]
    for g in range(FLASH_ROWS):
        rows = slice(g * tr, (g + 1) * tr)
        s = scores[g]
        m_old = m_ref[rows, :]
        m_new = jnp.maximum(m_old, jnp.max(s, axis=-1, keepdims=True))
        pm = jnp.exp2(s - m_new).astype(BF16)
        acc_ref[rows, :] = jnp.exp2(m_old - m_new) * acc_ref[rows, :] + _dot(pm, v)
        m_ref[rows, :] = m_new

    @pl.when(j == nkv - 1)
    def _():
        o_ref[...] = acc_ref[:, 0:V_HEAD] / acc_ref[:, V_HEAD:]


def _flash_attention(q, k, v):
    S = q.shape[1]
    tq = min(2048, S)
    tk = min(2048, S)
    nkv = S // tk
    return pl.pallas_call(
        functools.partial(_flash_kernel, nkv, tq),
        grid=(MLA_HEADS, S // tq, nkv),
        in_specs=[pl.BlockSpec((None, tq, MLA_QK), lambda h, i, j: (h, i, 0)),
                  pl.BlockSpec((None, tk, MLA_QK), lambda h, i, j: (h, j, 0)),
                  pl.BlockSpec((None, tk, 2 * V_HEAD), lambda h, i, j: (h, j, 0))],
        out_specs=pl.BlockSpec((tq, V_HEAD), lambda h, i, j: (i, h)),
        out_shape=jax.ShapeDtypeStruct((S, MLA_W), F32),
        scratch_shapes=[pltpu.VMEM((tq, 1), F32), pltpu.VMEM((tq, 2 * V_HEAD), F32)],
        compiler_params=_cparams(("parallel", "parallel", "arbitrary")),
        name="mla_flash",
    )(q, k, v)


def _merge_kernel(xn_ref, ya_ref, h0_ref, h1_ref, gb_ref, yc_ref, wg0_ref, wg1_ref, wg2_ref,
                  wa_ref, wb_ref, wc_ref, o_ref):
    xn = xn_ref[...]
    yb = (h0_ref[...] + h1_ref[...]) * jax.nn.gelu(gb_ref[...])
    m = _sigmoid(_dot(xn, wg0_ref[...])) * _dot(ya_ref[...].astype(BF16), wa_ref[...])
    m = m + _sigmoid(_dot(xn, wg1_ref[...])) * _dot(yb.astype(BF16), wb_ref[...])
    m = m + _sigmoid(_dot(xn, wg2_ref[...])) * _dot(yc_ref[...].astype(BF16), wc_ref[...])
    o_ref[...] = m.astype(o_ref.dtype)


def _merge(xn, ya, hs0, hs1, h_lru, yc, w_gates, l, p):
    S = ya.shape[0]
    tm = min(512, S)
    tn = 512
    nj = D_MODEL // tn
    rowa = pl.BlockSpec((tm, RWKV_W), lambda j, i: (i, 0))
    wgate = lambda b: pl.BlockSpec((D_MODEL, tn), lambda j, i: (0, b * nj + j))
    wcol = lambda k: pl.BlockSpec((k, tn), lambda j, i: (0, j))
    return pl.pallas_call(
        _merge_kernel,
        grid=(nj, S // tm),
        in_specs=[pl.BlockSpec((tm, D_MODEL), lambda j, i: (i, 0)),
                  rowa, rowa, rowa, pl.BlockSpec((tm, LRU_W), lambda j, i: (i, 1)),
                  pl.BlockSpec((tm, MLA_W), lambda j, i: (i, 0)),
                  wgate(0), wgate(1), wgate(2), wcol(RWKV_W), wcol(LRU_W), wcol(MLA_W)],
        out_specs=pl.BlockSpec((tm, tn), lambda j, i: (i, j)),
        out_shape=jax.ShapeDtypeStruct((S, D_MODEL), BF16),
        compiler_params=_cparams(("parallel", "parallel")),
        name="merge",
    )(xn, ya, hs0, hs1, h_lru, yc, w_gates, w_gates, w_gates,
      p['wo_rwkv'][l].astype(BF16), p['wo_lru'][l].astype(BF16), p['wo_mla'][l].astype(BF16))


def _router_kernel(x_ref, g_ref, wr_ref, xn_o, lg_o):
    x = x_ref[...]
    ms = jnp.mean(x * x, axis=-1, keepdims=True)
    xn = x * lax.rsqrt(ms + NORM_EPS) * g_ref[...]
    xn_o[...] = xn
    lg_o[...] = _dot3_nt(wr_ref[...], xn)


def _router(x, gain, w_router):
    S = x.shape[0]
    tm = min(512, S)
    return pl.pallas_call(
        _router_kernel,
        grid=(S // tm,),
        in_specs=[pl.BlockSpec((tm, D_MODEL), lambda i: (i, 0)), pl.BlockSpec((1, D_MODEL), lambda i: (0, 0)),
                  pl.BlockSpec((N_EXPERTS, D_MODEL), lambda i: (0, 0))],
        out_specs=[pl.BlockSpec((tm, D_MODEL), lambda i: (i, 0)), pl.BlockSpec((N_EXPERTS, tm), lambda i: (0, i))],
        out_shape=[jax.ShapeDtypeStruct((S, D_MODEL), F32), jax.ShapeDtypeStruct((N_EXPERTS, S), F32)],
        compiler_params=_cparams(("parallel",)),
        name="moe_router",
    )(x, gain.reshape(1, D_MODEL), w_router.T)


def _route_kernel(nb, cap, lg_ref, idx_o, gate_o, aff_ref):
    E = N_EXPERTS
    m = lg_ref[0]
    for e in range(1, E):
        m = jnp.maximum(m, lg_ref[e])
    z = jnp.zeros_like(m)
    for e in range(E):
        z = z + jnp.exp(lg_ref[e] - m)
    for e in range(E):
        aff_ref[e] = jnp.exp(lg_ref[e] - m) / z

    li = lax.broadcasted_iota(jnp.int32, (LANES, LANES), 0)
    lj = lax.broadcasted_iota(jnp.int32, (LANES, LANES), 1)
    upper = jnp.where(li <= lj, 1.0, 0.0).astype(BF16)
    bi = lax.broadcasted_iota(jnp.int32, (nb, nb), 0)
    bj = lax.broadcasted_iota(jnp.int32, (nb, nb), 1)
    lstrict = jnp.where(bj < bi, 1.0, 0.0).astype(BF16)
    qrow = lax.broadcasted_iota(jnp.int32, (1, cap), 1).astype(F32)
    brow = lax.broadcasted_iota(jnp.int32, (nb, cap), 0).astype(F32)
    irow = lax.broadcasted_iota(jnp.int32, (LANES, cap), 0).astype(F32)

    def prefix(mask):
        cum = _dot(mask.astype(BF16), upper)
        tot = cum[:, LANES - 1:LANES]
        off = _dot(lstrict, jnp.broadcast_to(tot, (nb, LANES)).astype(BF16))
        return cum, off, tot

    def expert(e, carry):
        aff = aff_ref[e]
        bits = pltpu.bitcast(aff, jnp.int32)

        def bisect(_, lohi):
            lo, hi = lohi
            mid = lo + lax.shift_right_arithmetic(hi - lo, 1)
            cnt = jnp.sum(jnp.where(bits >= mid, 1.0, 0.0), keepdims=True)
            ok = cnt >= cap
            return jnp.where(ok, mid, lo), jnp.where(ok, hi, mid)

        lo0 = jnp.zeros((1, 1), jnp.int32)
        hi0 = jnp.full((1, 1), 0x7F800000, jnp.int32)
        thr, _ = lax.fori_loop(0, 31, bisect, (lo0, hi0))
        gt = jnp.where(bits > thr, 1.0, 0.0)
        eq = jnp.where(bits == thr, 1.0, 0.0)
        need = cap - jnp.sum(gt, keepdims=True)
        ecum, eoff, _ = prefix(eq)
        sel = gt + eq * jnp.where(ecum + eoff <= need, 1.0, 0.0)
        cum, off, tot = prefix(sel)
        full = jnp.where(off[:, 0:1] + tot <= qrow, 1.0, 0.0)
        nfull = jnp.sum(full, axis=0, keepdims=True)
        offsel = jnp.sum(full * tot, axis=0, keepdims=True)
        onehot_b = jnp.where(brow == nfull, 1.0, 0.0).astype(BF16)
        cum_t = _dot_tn(cum.astype(BF16), onehot_b)
        within = jnp.sum(jnp.where(cum_t <= qrow - offsel, 1.0, 0.0), axis=0, keepdims=True)
        idx_o[e] = (nfull * LANES + within).astype(jnp.int32)
        a1 = aff.astype(BF16)
        r1 = aff - a1.astype(F32)
        a2 = r1.astype(BF16)
        a3 = (r1 - a2.astype(F32)).astype(BF16)
        aff_t = _dot_tn(a1, onehot_b) + _dot_tn(a2, onehot_b) + _dot_tn(a3, onehot_b)
        gate_o[e] = jnp.sum(jnp.where(irow == within, aff_t, 0.0), axis=0, keepdims=True)
        return carry

    lax.fori_loop(0, E, expert, 0)


def _route(logits_t, cap):
    E, S = logits_t.shape
    nb = S // LANES
    return pl.pallas_call(
        functools.partial(_route_kernel, nb, cap),
        out_shape=[jax.ShapeDtypeStruct((E, 1, cap), jnp.int32), jax.ShapeDtypeStruct((E, 1, cap), F32)],
        scratch_shapes=[pltpu.VMEM((E, nb, LANES), F32)],
        compiler_params=pltpu.CompilerParams(vmem_limit_bytes=VMEM_LIMIT),
        name="moe_route",
    )(logits_t.reshape(E, nb, LANES))


DMA_GROUPS = 4


def _expert_kernel(tq, nsteps, idx_ref, xacc_hbm, xn_hbm, gate_ref, wg_ref, wu_ref, wd_ref, out_hbm,
                   xs_ref, ob_ref, tok_sems, row_sem, sct_sem):
    del xacc_hbm
    n = pl.program_id(0) * pl.num_programs(1) + pl.program_id(1)
    slot = lax.rem(n, 2)

    def hbm_row(ref, tok):
        return ref.at[lax.shift_right_logical(tok, 3), pl.ds(lax.bitwise_and(tok, SUBLANES - 1), 1), :]

    def token_copy(step, buf, i, u):
        tok = idx_ref[step * tq + i * SUBLANES + u]
        return pltpu.make_async_copy(hbm_row(xn_hbm, tok), xs_ref.at[buf, i, pl.ds(u, 1), :], tok_sems.at[buf])

    def row_copy(step, i, u):
        tok = idx_ref[step * tq + i * SUBLANES + u]
        return pltpu.make_async_copy(hbm_row(out_hbm, tok), ob_ref.at[i, pl.ds(u, 1), :], row_sem.at[0])

    def scatter_copy(step, i, u):
        tok = idx_ref[step * tq + i * SUBLANES + u]
        return pltpu.make_async_copy(ob_ref.at[i, pl.ds(u, 1), :], hbm_row(out_hbm, tok), sct_sem.at[0])

    def for_rows(fn):
        def body(i2, c):
            for g in range(DMA_GROUPS):
                for u in range(SUBLANES):
                    fn(i2 * DMA_GROUPS + g, u)
            return c
        lax.fori_loop(0, tq // (SUBLANES * DMA_GROUPS), body, 0)

    @pl.when(n == 0)
    def _():
        for_rows(lambda i, u: token_copy(0, 0, i, u).start())

    @pl.when(n + 1 < nsteps)
    def _():
        for_rows(lambda i, u: token_copy(n + 1, 1 - slot, i, u).start())

    for_rows(lambda i, u: token_copy(n, slot, i, u).wait())
    xs = xs_ref[slot].reshape(tq, D_MODEL).astype(BF16)
    hg = _dot(xs, wg_ref[...])
    hu = _dot(xs, wu_ref[...])
    hdn = ((hg * _sigmoid(hg)) * hu).astype(BF16)

    @pl.when(n > 0)
    def _():
        for_rows(lambda i, u: scatter_copy(n - 1, i, u).wait())

    for_rows(lambda i, u: row_copy(n, i, u).start())
    ye = _dot(hdn, wd_ref[...])
    for_rows(lambda i, u: row_copy(n, i, u).wait())
    upd = ob_ref[...].reshape(tq, D_MODEL) + ye * gate_ref[...]
    ob_ref[...] = upd.reshape(tq // SUBLANES, SUBLANES, D_MODEL)
    for_rows(lambda i, u: scatter_copy(n, i, u).start())

    @pl.when(n == nsteps - 1)
    def _():
        for_rows(lambda i, u: scatter_copy(n, i, u).wait())


def _experts(x, xn, idx, gate, l, p):
    S = x.shape[0]
    E, cap = idx.shape
    tq = min(512, cap)
    nt = cap // tq
    grid_spec = pltpu.PrefetchScalarGridSpec(
        num_scalar_prefetch=1,
        grid=(E, nt),
        in_specs=[
            pl.BlockSpec(memory_space=pl.ANY),
            pl.BlockSpec(memory_space=pl.ANY),
            pl.BlockSpec((None, tq, 1), lambda e, t, idx: (e, t, 0)),
            pl.BlockSpec((None, D_MODEL, EXPERT_FF), lambda e, t, idx: (e, 0, 0)),
            pl.BlockSpec((None, D_MODEL, EXPERT_FF), lambda e, t, idx: (e, 0, 0)),
            pl.BlockSpec((None, EXPERT_FF, D_MODEL), lambda e, t, idx: (e, 0, 0)),
        ],
        out_specs=pl.BlockSpec(memory_space=pl.ANY),
        scratch_shapes=[pltpu.VMEM((2, tq // SUBLANES, SUBLANES, D_MODEL), F32),
                        pltpu.VMEM((tq // SUBLANES, SUBLANES, D_MODEL), F32),
                        pltpu.SemaphoreType.DMA((2,)), pltpu.SemaphoreType.DMA((1,)),
                        pltpu.SemaphoreType.DMA((1,))],
    )
    grouped = (S // SUBLANES, SUBLANES, D_MODEL)
    out = pl.pallas_call(
        functools.partial(_expert_kernel, tq, E * nt),
        grid_spec=grid_spec,
        out_shape=jax.ShapeDtypeStruct(grouped, F32),
        input_output_aliases={1: 0},
        compiler_params=_cparams(("arbitrary", "arbitrary")),
        name="moe_experts",
    )(idx.reshape(E * cap), x.reshape(grouped), xn.reshape(grouped), gate.reshape(E, cap, 1),
      p['w_gate'][l].astype(BF16), p['w_up'][l].astype(BF16), p['w_down'][l].astype(BF16))
    return out.reshape(S, D_MODEL)


def _trunk(x, positions, p):
    S = x.shape[0]
    cap = CAPACITY_FACTOR * S // N_EXPERTS
    depth = p['w_in'].shape[0]
    v_first = None
    for l in range(depth):
        xn = _rmsnorm(x, p['norm_mix'][l], BF16)
        h_rwkv, h_lru, h_mla, w_gates = _in_projection(xn, p['w_in'][l], p['rwkv_vres_down'][l - 1] if l > 0 else None)
        r, v, kk, lw, kd, a, g, bonus = _rwkv_prep(h_rwkv, h_mla, l, p, v_first)
        if l == 0:
            v_first = v
        yf, yb = _rwkv_scan(r, v, kk, lw, kd, a)
        ya = _rwkv_post(yf, yb, bonus, g, p['rwkv_lnx_w'][l], p['rwkv_lnx_b'][l])
        hs0 = _lru_scan(h_lru, l, p, 0)
        hs1 = _lru_scan(h_lru, l, p, 1)
        q, k, vv = _mla_prep(h_mla, positions, l, p)
        yc = _flash_attention(q, k, vv)
        merged = _merge(xn, ya, hs0, hs1, h_lru, yc, w_gates, l, p)
        x = _matmul(merged, p['w_out'][l].astype(BF16), D_MODEL, res=x, name="outproj")
        xn2, logits_t = _router(x, p['norm_ffn'][l], p['w_router'][l])
        idx, gate = _route(logits_t, cap)
        x = _experts(x, xn2, idx.reshape(N_EXPERTS, cap), gate.reshape(N_EXPERTS, cap), l, p)
    return _rmsnorm(x, p['norm_final'], F32)


def kernel(x, positions, norm_mix, w_in, rwkv_shift_mu, rwkv_w0, rwkv_w2, rwkv_a0, rwkv_a2, rwkv_g2, rwkv_k_k,
           rwkv_k_a, rwkv_r_k, rwkv_lnx_w, rwkv_lnx_b, rwkv_vres_down, rwkv_vres_up, rwkv_vres_b, lru_conv_w,
           lru_conv_b, lru_wa, lru_ba, lru_wx, lru_bx, lru_lambda, mla_q_norm, mla_kv_norm, mla_w_uq, mla_w_uk,
           mla_w_uv, wo_rwkv, wo_lru, wo_mla, w_out, norm_ffn, w_router, w_gate, w_up, w_down, norm_final):
    p = dict(norm_mix=norm_mix, w_in=w_in, rwkv_shift_mu=rwkv_shift_mu, rwkv_w0=rwkv_w0, rwkv_w2=rwkv_w2,
             rwkv_a0=rwkv_a0, rwkv_a2=rwkv_a2, rwkv_g2=rwkv_g2, rwkv_k_k=rwkv_k_k, rwkv_k_a=rwkv_k_a,
             rwkv_r_k=rwkv_r_k, rwkv_lnx_w=rwkv_lnx_w, rwkv_lnx_b=rwkv_lnx_b, rwkv_vres_down=rwkv_vres_down,
             rwkv_vres_up=rwkv_vres_up, rwkv_vres_b=rwkv_vres_b, lru_conv_w=lru_conv_w, lru_conv_b=lru_conv_b,
             lru_wa=lru_wa, lru_ba=lru_ba, lru_wx=lru_wx, lru_bx=lru_bx, lru_lambda=lru_lambda,
             mla_q_norm=mla_q_norm, mla_kv_norm=mla_kv_norm, mla_w_uq=mla_w_uq, mla_w_uk=mla_w_uk,
             mla_w_uv=mla_w_uv, wo_rwkv=wo_rwkv, wo_lru=wo_lru, wo_mla=wo_mla, w_out=w_out, norm_ffn=norm_ffn,
             w_router=w_router, w_gate=w_gate, w_up=w_up, w_down=w_down, norm_final=norm_final)
    B, S, D = x.shape
    outs = [_trunk(x[b], positions[b], p) for b in range(B)]
    return jnp.stack(outs, axis=0)
```

```python
import functools

import numpy as np
import jax
import jax.numpy as jnp
from jax import lax
from jax.experimental import pallas as pl
from jax.experimental.pallas import tpu as pltpu

F32 = jnp.float32
BF16 = jnp.bfloat16

D_MODEL = 2048
RWKV_HEADS = 12
RWKV_HEAD = 64
RWKV_W = RWKV_HEADS * RWKV_HEAD
DECAY_LORA = 64
ICLR_LORA = 64
VRES_LORA = 32
GATE_LORA = 128
GN_EPS = 64e-5
LRU_BLOCKS = 6
LRU_BLOCK = 128
LRU_W = LRU_BLOCKS * LRU_BLOCK
CONV_W = 4
LRU_C = 8.0
MLA_HEADS = 4
QK_NOPE = 128
QK_ROPE = 64
V_HEAD = 128
Q_LORA = 384
KV_LORA = 128
MLA_W = MLA_HEADS * V_HEAD
ROPE_THETA = 10000.0
N_BRANCH = 3
N_EXPERTS = 16
EXPERT_FF = 1024
CAPACITY_FACTOR = 2
NORM_EPS = 1e-6
RWKV_COLS = 3 * RWKV_W + 2 * DECAY_LORA + 2 * ICLR_LORA + GATE_LORA

LANES = 128
SUBLANES = 8
VMEM_LIMIT = 56 * 1024 * 1024

SEG_LRU = RWKV_COLS
SEG_MLA = SEG_LRU + 2 * LRU_W
SEG_GATES = SEG_MLA + Q_LORA + KV_LORA + QK_ROPE
MLA_SEG_W = Q_LORA + KV_LORA + 2 * LANES
MLA_COL_KR = Q_LORA + KV_LORA
MLA_COL_VRES = MLA_COL_KR + LANES

CHUNK = 64
PAIR = 2 * RWKV_HEAD
NPAIR = RWKV_W // PAIR


def _cparams(sem):
    return pltpu.CompilerParams(dimension_semantics=sem, vmem_limit_bytes=VMEM_LIMIT)


def _dot(a, b):
    return jnp.dot(a, b, preferred_element_type=F32)


def _dot_nt(a, b):
    return lax.dot_general(a, b, (((1,), (1,)), ((), ())), preferred_element_type=F32)


def _dot_tn(a, b):
    return lax.dot_general(a, b, (((0,), (0,)), ((), ())), preferred_element_type=F32)


def _split(x):
    hi = x.astype(BF16)
    lo = (x - hi.astype(F32)).astype(BF16)
    return hi, lo


def _dot_exactb(x, b):
    hi, lo = _split(x)
    return _dot(hi, b) + _dot(lo, b)


def _dot3(a, b):
    ah, al = _split(a)
    bh, bl = _split(b)
    return _dot(ah, bh) + (_dot(ah, bl) + _dot(al, bh))


def _dot3_nt(a, b):
    ah, al = _split(a)
    bh, bl = _split(b)
    return _dot_nt(ah, bh) + (_dot_nt(ah, bl) + _dot_nt(al, bh))


def _dot3_tn(a, b):
    ah, al = _split(a)
    bh, bl = _split(b)
    return _dot_tn(ah, bh) + (_dot_tn(ah, bl) + _dot_tn(al, bh))


def _sigmoid(x):
    return 1.0 / (1.0 + jnp.exp(-x))


def _softplus(x):
    return jnp.maximum(x, 0.0) + jnp.log(1.0 + jnp.exp(-jnp.abs(x)))


def _rmsnorm_kernel(x_ref, g_ref, o_ref):
    x = x_ref[...]
    ms = jnp.mean(x * x, axis=-1, keepdims=True)
    o_ref[...] = (x * lax.rsqrt(ms + NORM_EPS) * g_ref[...]).astype(o_ref.dtype)


def _rmsnorm(x, gain, dtype):
    S, D = x.shape
    tm = min(512, S)
    return pl.pallas_call(
        _rmsnorm_kernel,
        grid=(S // tm,),
        in_specs=[pl.BlockSpec((tm, D), lambda i: (i, 0)), pl.BlockSpec((1, D), lambda i: (0, 0))],
        out_specs=pl.BlockSpec((tm, D), lambda i: (i, 0)),
        out_shape=jax.ShapeDtypeStruct((S, D), dtype),
        compiler_params=_cparams(("parallel",)),
        name="rmsnorm",
    )(x, gain.reshape(1, D))


def _matmul_kernel(has_res, *refs):
    if has_res:
        a_ref, w_ref, res_ref, o_ref = refs
        o_ref[...] = res_ref[...] + _dot(a_ref[...], w_ref[...])
    else:
        a_ref, w_ref, o_ref = refs
        o_ref[...] = _dot(a_ref[...], w_ref[...])


def _matmul(a, w, tn, res=None, name="matmul"):
    M, K = a.shape
    N = w.shape[1]
    tm = min(512, M)
    in_specs = [pl.BlockSpec((tm, K), lambda j, i: (i, 0)), pl.BlockSpec((K, tn), lambda j, i: (0, j))]
    args = [a, w]
    if res is not None:
        in_specs.append(pl.BlockSpec((tm, tn), lambda j, i: (i, j)))
        args.append(res)
    return pl.pallas_call(
        functools.partial(_matmul_kernel, res is not None),
        grid=(N // tn, M // tm),
        in_specs=in_specs,
        out_specs=pl.BlockSpec((tm, tn), lambda j, i: (i, j)),
        out_shape=jax.ShapeDtypeStruct((M, N), F32),
        compiler_params=_cparams(("parallel", "parallel")),
        name=name,
    )(*args)


def _in_projection(xn, w_in, vres_down):
    w = w_in.astype(BF16)
    w_mla = jnp.zeros((D_MODEL, MLA_SEG_W), BF16)
    w_mla = w_mla.at[:, :MLA_COL_KR + QK_ROPE].set(w[:, SEG_MLA:SEG_GATES])
    if vres_down is not None:
        w_mla = w_mla.at[:, MLA_COL_VRES:MLA_COL_VRES + VRES_LORA].set(vres_down.astype(BF16))
    h_rwkv = _matmul(xn, w[:, :SEG_LRU], RWKV_COLS, name="inproj_rwkv")
    h_lru = _matmul(xn, w[:, SEG_LRU:SEG_MLA], 2 * LRU_W, name="inproj_lru")
    h_mla = _matmul(xn, w_mla, MLA_SEG_W, name="inproj_mla")
    return h_rwkv, h_lru, h_mla, w[:, SEG_GATES:]


def _rwkv_prep_kernel(has_vres, tm, nblk, *refs):
    if has_vres:
        (h_ref, hp_ref, hn_ref, lora_ref, vfirst_ref, vup_ref, vb_ref, *rest) = refs
    else:
        (h_ref, hp_ref, hn_ref, *rest) = refs
    (mu_ref, w0_ref, w2_ref, a0_ref, a2_ref, g2_ref, kkw_ref, ka_ref, rk_ref, bd_ref,
     r_o, v_o, kk_o, lw_o, kd_o, a_o, g_o, bonus_o) = rest
    i = pl.program_id(0)
    p = h_ref[...]
    rows = lax.broadcasted_iota(jnp.int32, (tm, 1), 0)
    prev_row = jnp.where(i == 0, 0.0, hp_ref[SUBLANES - 1:SUBLANES, :])
    next_row = jnp.where(i == nblk - 1, 0.0, hn_ref[0:1, :])
    prev = jnp.where(rows == 0, prev_row, pltpu.roll(p, 1, 0))
    nxt = jnp.where(rows == tm - 1, next_row, pltpu.roll(p, tm - 1, 0))
    hs = p + mu_ref[0:1, :] * (prev - p) + mu_ref[1:2, :] * (nxt - p)

    W = RWKV_W
    r = hs[:, 0:W]
    k = hs[:, W:2 * W]
    v = hs[:, 2 * W:3 * W]
    wd = hs[:, 3 * W:3 * W + LANES]
    ad = hs[:, 3 * W + LANES:3 * W + 2 * LANES]
    gd = hs[:, 3 * W + 2 * LANES:3 * W + 3 * LANES]
    def put(ref, val, *lead):
        for pr in range(NPAIR):
            ref[(*lead, pr)] = val[:, pr * PAIR:(pr + 1) * PAIR]

    if has_vres:
        vg = _sigmoid(vb_ref[...] + _dot(lora_ref[...].astype(BF16), vup_ref[...]))
        v_first = jnp.concatenate([vfirst_ref[pr] for pr in range(NPAIR)], axis=1)
        v = v + (v_first - v) * vg
    bd = bd_ref[...]
    kk = k * kkw_ref[...]
    n2 = _dot_exactb(kk * kk, bd)
    kk = kk / jnp.maximum(jnp.sqrt(n2), 1e-12)
    tw = jnp.tanh(wd).astype(BF16)
    adb = ad.astype(BF16)
    ksum = None
    for d in range(2):
        wl = w0_ref[d:d + 1, :] + _dot(tw, w2_ref[d])
        w_log = -_softplus(-wl) - 0.5
        put(lw_o, -jnp.exp(w_log), d)
        a = _sigmoid(a0_ref[d:d + 1, :] + _dot(adb, a2_ref[d]))
        kd = k * (1.0 + (a - 1.0) * ka_ref[...])
        put(a_o, a, d)
        put(kd_o, kd, d)
        ksum = kd if ksum is None else ksum + kd
    put(r_o, r)
    put(v_o, v)
    put(kk_o, kk)
    g_o[...] = _dot(_sigmoid(gd).astype(BF16), g2_ref[...])
    bonus_o[...] = _dot_exactb(r * ksum * rk_ref[...], bd) * v


def _head_blockdiag():
    hid = np.arange(RWKV_W) // RWKV_HEAD
    return jnp.asarray((hid[:, None] == hid[None, :]).astype(np.float32), dtype=BF16)


def _rwkv_prep(h, h_mla, l, p, v_first):
    S = h.shape[0]
    tm = min(256, S)
    nblk = S // tm
    has_vres = l > 0
    W = RWKV_W
    hb = tm // SUBLANES
    nrow8 = S // SUBLANES

    def zpad(w, rows_before):
        z = jnp.zeros((LANES, W), F32)
        return z.at[rows_before:rows_before + w.shape[0]].set(w)

    w2 = jnp.stack([zpad(p['rwkv_w2'][l, 0], 0), zpad(p['rwkv_w2'][l, 1], DECAY_LORA)]).astype(BF16)
    a2 = jnp.stack([zpad(p['rwkv_a2'][l, 0], 0), zpad(p['rwkv_a2'][l, 1], ICLR_LORA)]).astype(BF16)
    const2 = lambda shape: pl.BlockSpec(shape, lambda i: (0,) * len(shape))
    in_specs = [
        pl.BlockSpec((tm, RWKV_COLS), lambda i: (i, 0)),
        pl.BlockSpec((SUBLANES, RWKV_COLS), lambda i: (jnp.maximum(i * hb - 1, 0), 0)),
        pl.BlockSpec((SUBLANES, RWKV_COLS), lambda i: (jnp.minimum((i + 1) * hb, nrow8 - 1), 0)),
    ]
    args = [h, h, h]
    if has_vres:
        in_specs += [
            pl.BlockSpec((tm, LANES), lambda i: (i, MLA_COL_VRES // LANES)),
            pl.BlockSpec((NPAIR, tm, PAIR), lambda i: (0, i, 0)),
            const2((LANES, W)),
            const2((1, W)),
        ]
        vup = jnp.zeros((LANES, W), F32).at[:VRES_LORA].set(p['rwkv_vres_up'][l - 1]).astype(BF16)
        args += [h_mla, v_first, vup, p['rwkv_vres_b'][l - 1].reshape(1, W)]
    in_specs += [
        const2((2, RWKV_COLS)), const2((2, W)), const2((2, LANES, W)), const2((2, W)), const2((2, LANES, W)),
        const2((LANES, W)), const2((1, W)), const2((1, W)), const2((1, W)), const2((W, W)),
    ]
    args += [
        p['rwkv_shift_mu'][l], p['rwkv_w0'][l], w2, p['rwkv_a0'][l], a2,
        p['rwkv_g2'][l].astype(BF16), p['rwkv_k_k'][l].reshape(1, W), p['rwkv_k_a'][l].reshape(1, W),
        p['rwkv_r_k'][l].reshape(1, W), _head_blockdiag(),
    ]
    row = pl.BlockSpec((tm, W), lambda i: (i, 0))
    prow = pl.BlockSpec((NPAIR, tm, PAIR), lambda i: (0, i, 0))
    prow2 = pl.BlockSpec((2, NPAIR, tm, PAIR), lambda i: (0, 0, i, 0))
    sds = jax.ShapeDtypeStruct((S, W), F32)
    psds = jax.ShapeDtypeStruct((NPAIR, S, PAIR), F32)
    psds2 = jax.ShapeDtypeStruct((2, NPAIR, S, PAIR), F32)
    return pl.pallas_call(
        functools.partial(_rwkv_prep_kernel, has_vres, tm, nblk),
        grid=(nblk,),
        in_specs=in_specs,
        out_specs=[prow, prow, prow, prow2, prow2, prow2, row, row],
        out_shape=[psds, psds, psds, psds2, psds2, psds2, sds, sds],
        compiler_params=_cparams(("parallel",)),
        name="rwkv_prep",
    )(*args)


N_MASK = 8


def _scan_masks():
    n = PAIR
    out = np.zeros((2, N_MASK, n, n), np.float32)
    r = np.arange(n)[:, None]
    c = np.arange(n)[None, :]
    same = (r // CHUNK) == (c // CHUNK)
    for d in range(2):
        before = (c < r) if d == 0 else (c > r)
        strict = same & before
        out[d, 0] = strict
        out[d, 1] = same & (before | (c == r))
        out[d, 2] = strict & ((r // 8) == (c // 8))
        for j, b in enumerate((8, 16, 32)):
            out[d, 3 + j] = strict & ((r // (2 * b)) == (c // (2 * b))) & ((r // b) != (c // b))
        out[d, 6] = (r == c)
        out[d, 7] = (before | (c == r)) & (r < CHUNK) & (c < CHUNK)
    return jnp.asarray(out)


def _mm(a, b):
    return _dot(a.astype(BF16), b.astype(BF16))


def _mm_nt(a, b):
    return _dot_nt(a.astype(BF16), b.astype(BF16))


def _mm_tn(a, b):
    return _dot_tn(a.astype(BF16), b.astype(BF16))


def _rwkv_scan_kernel(nsub, rf, vf, kkf, lwf, kf, af, rb, vb, kkb, lwb, kb_, ab, m_ref, yf, yb, s_ref):
    T = CHUNK
    dirs = ((rf, vf, kkf, lwf, kf, af, yf), (rb, vb, kkb, lwb, kb_, ab, yb))

    @pl.when(pl.program_id(1) == 0)
    def _():
        s_ref[...] = jnp.zeros_like(s_ref)

    lane = lax.broadcasted_iota(jnp.int32, (1, PAIR), 1)
    m0 = jnp.where(lane < RWKV_HEAD, 1.0, 0.0)
    m1 = 1.0 - m0

    def stack(x):
        return jnp.concatenate([x * m0, x * m1], axis=0)

    def local(d, j):
        r_ref, v_ref, kk_ref, lw_ref, k_ref, a_ref, _ = dirs[d]
        sl = slice(j * T, (j + 1) * T)
        strict, incl, eye = m_ref[d, 0], m_ref[d, 1], m_ref[d, 6]
        lw = lw_ref[sl, :]
        kk = kk_ref[sl, :]
        beta = a_ref[sl, :] * kk
        kd = k_ref[sl, :]
        cs = m_ref[d, 7][0:T, 0:T].astype(BF16)
        lh, ll = _split(lw)
        c = _dot(cs, lh) + _dot(cs, ll)
        c_tot = c[0:1, :] if d == 1 else c[T - 1:T, :]
        e_inv = jnp.exp(-c)
        e_fin = jnp.exp(c_tot - c)
        kt = stack(kk * jnp.exp(c - lw))
        rt = stack(r_ref[sl, :] * jnp.exp(c))
        bb = stack(beta * e_inv)
        kb = stack(kd * e_inv)
        bh = stack(beta * e_fin)
        kh = stack(kd * e_fin)
        vs = stack(v_ref[sl, :])

        yield
        p1 = _mm_nt(jnp.concatenate([kt, rt], axis=0), jnp.concatenate([bb, kb], axis=0))
        A = p1[0:PAIR, 0:PAIR] * strict
        B = p1[0:PAIR, PAIR:] * strict
        C = p1[PAIR:, 0:PAIR] * incl
        E = p1[PAIR:, PAIR:] * incl
        yield

        d8 = A * m_ref[d, 2]
        d2 = _mm(d8, d8)
        bv = _mm(B, vs)
        yield
        d4 = _mm(d2, d2)
        x = _mm(eye - d8, eye + d2)
        yield
        x = _mm(x, eye + d4)
        yield
        for lvl in range(3):
            aoff = A * m_ref[d, 3 + lvl]
            ax = _mm(aoff, x)
            yield
            x = x - _mm(x, ax)
            yield

        mu = -_mm(x, jnp.concatenate([kt, bv], axis=1))
        ev = _mm(E, vs)
        hv = _mm_tn(kh, vs)
        yield
        cm = _mm(C, mu)
        gh = _mm_tn(bh, mu)
        yield
        rm = rt + cm[:, 0:PAIR]
        y0 = cm[:, PAIR:] + ev
        p_col = jnp.sum(eye * jnp.exp(c_tot), axis=1, keepdims=True)
        Gp = gh[:, 0:PAIR].astype(BF16)
        H = gh[:, PAIR:] + hv
        return rm.astype(BF16), y0, p_col, Gp, H

    keys = [(d, j) for d in range(2) for j in range(nsub)]
    gens = [local(d, j) for d, j in keys]
    loc = {}
    while len(loc) < len(keys):
        for key, gen in zip(keys, gens):
            try:
                next(gen)
            except StopIteration as done:
                loc[key] = done.value
    s = [s_ref[0], s_ref[1]]
    for step in range(nsub):
        for d in range(2):
            j = step if d == 0 else nsub - 1 - step
            rm, y0, p_col, Gp, H = loc[d, j]
            sb = s[d].astype(BF16)
            yst = _dot(rm, sb) + y0
            dirs[d][6][j * T:(j + 1) * T, :] = yst[0:T, :] + yst[T:, :]
            s[d] = p_col * s[d] + (_dot(Gp, sb) + H)
    s_ref[0] = s[0]
    s_ref[1] = s[1]


def _rwkv_scan(r, v, kk, lw, kd, a):
    S = r.shape[1]
    tb = min(1024, S)
    nsub = tb // CHUNK
    nblk = S // tb
    fwd = pl.BlockSpec((None, tb, PAIR), lambda pr, c: (pr, c, 0))
    bwd = pl.BlockSpec((None, tb, PAIR), lambda pr, c: (pr, nblk - 1 - c, 0))
    fwd2 = pl.BlockSpec((None, None, tb, PAIR), lambda pr, c: (0, pr, c, 0))
    bwd2 = pl.BlockSpec((None, None, tb, PAIR), lambda pr, c: (1, pr, nblk - 1 - c, 0))
    sds = jax.ShapeDtypeStruct((NPAIR, S, PAIR), F32)
    return pl.pallas_call(
        functools.partial(_rwkv_scan_kernel, nsub),
        grid=(NPAIR, nblk),
        in_specs=[fwd, fwd, fwd, fwd2, fwd2, fwd2, bwd, bwd, bwd, bwd2, bwd2, bwd2,
                  pl.BlockSpec((2, N_MASK, PAIR, PAIR), lambda pr, c: (0, 0, 0, 0))],
        out_specs=[fwd, bwd],
        out_shape=[sds, sds],
        scratch_shapes=[pltpu.VMEM((2, PAIR, PAIR), F32)],
        compiler_params=_cparams(("parallel", "arbitrary")),
        name="rwkv_scan",
    )(r, v, kk, lw, kd, a, r, v, kk, lw, kd, a, _scan_masks())


def _rwkv_post_kernel(yf_ref, yb_ref, bonus_ref, g_ref, w_ref, b_ref, bd_ref, o_ref):
    y = jnp.concatenate([yf_ref[pr] + yb_ref[pr] for pr in range(NPAIR)], axis=1)
    bd = bd_ref[...]
    mu = _dot_exactb(y, bd) * (1.0 / RWKV_HEAD)
    yc = y - mu
    var = _dot_exactb(yc * yc, bd) * (1.0 / RWKV_HEAD)
    yn = yc * lax.rsqrt(var + GN_EPS) * w_ref[...] + b_ref[...]
    o_ref[...] = (yn + bonus_ref[...]) * g_ref[...]


def _rwkv_post(yf, yb, bonus, g, lnx_w, lnx_b):
    S = bonus.shape[0]
    tm = min(512, S)
    W = RWKV_W
    row = pl.BlockSpec((tm, W), lambda i: (i, 0))
    prow = pl.BlockSpec((NPAIR, tm, PAIR), lambda i: (0, i, 0))
    return pl.pallas_call(
        _rwkv_post_kernel,
        grid=(S // tm,),
        in_specs=[prow, prow, row, row,
                  pl.BlockSpec((1, W), lambda i: (0, 0)), pl.BlockSpec((1, W), lambda i: (0, 0)),
                  pl.BlockSpec((W, W), lambda i: (0, 0))],
        out_specs=row,
        out_shape=jax.ShapeDtypeStruct((S, W), F32),
        compiler_params=_cparams(("parallel",)),
        name="rwkv_post",
    )(yf, yb, bonus, g, lnx_w.reshape(1, W), lnx_b.reshape(1, W), _head_blockdiag())


def _lru_kernel(rev, tm, nblk, x_ref, xp_ref, xn_ref, cw_ref, cb_ref, wa_ref, ba_ref, wx_ref, bx_ref, lam_ref,
                o_ref, carry_ref):
    c = pl.program_id(0)
    ib = nblk - 1 - c if rev else c

    @pl.when(c == 0)
    def _():
        carry_ref[...] = jnp.zeros_like(carry_ref)

    x = x_ref[...]
    rows = lax.broadcasted_iota(jnp.int32, (tm, 1), 0)
    prev_row = jnp.where(ib == 0, 0.0, xp_ref[SUBLANES - 1:SUBLANES, :])
    nx0 = jnp.where(ib == nblk - 1, 0.0, xn_ref[0:1, :])
    nx1 = jnp.where(ib == nblk - 1, 0.0, xn_ref[1:2, :])
    xm1 = jnp.where(rows == 0, prev_row, pltpu.roll(x, 1, 0))
    xp1 = jnp.where(rows == tm - 1, nx0, pltpu.roll(x, tm - 1, 0))
    xp2 = jnp.where(rows == tm - 2, nx0, jnp.where(rows == tm - 1, nx1, pltpu.roll(x, tm - 2, 0)))
    xc = (cw_ref[0:1, :] * xm1 + cw_ref[1:2, :] * x + cw_ref[2:3, :] * xp1 + cw_ref[3:4, :] * xp2) + cb_ref[...]
    xcb = xc.astype(BF16)
    ra, ia = [], []
    for g in range(LRU_BLOCKS):
        blk = xcb[:, g * LRU_BLOCK:(g + 1) * LRU_BLOCK]
        ra.append(_dot(blk, wa_ref[g]))
        ia.append(_dot(blk, wx_ref[g]))
    rg = _sigmoid(jnp.concatenate(ra, axis=1) + ba_ref[...])
    ig = _sigmoid(jnp.concatenate(ia, axis=1) + bx_ref[...])
    log_a = -LRU_C * rg * _softplus(-lam_ref[...])
    a = jnp.exp(log_a)
    b = jnp.sqrt(1.0 - jnp.exp(2.0 * log_a)) * (ig * xc)
    k = 1
    while k < tm:
        if rev:
            ap, bp = pltpu.roll(a, tm - k, 0), pltpu.roll(b, tm - k, 0)
            valid = rows < tm - k
        else:
            ap, bp = pltpu.roll(a, k, 0), pltpu.roll(b, k, 0)
            valid = rows >= k
        b = jnp.where(valid, a * bp + b, b)
        a = jnp.where(valid, a * ap, a)
        k *= 2
    h = a * carry_ref[0:1, :] + b
    o_ref[...] = h
    last = h[0:1, :] if rev else h[tm - 1:tm, :]
    carry_ref[...] = jnp.broadcast_to(last, carry_ref.shape)


def _lru_scan(h_lru, l, p, d):
    S = h_lru.shape[0]
    tm = min(512, S)
    nblk = S // tm
    W = LRU_W
    rev = d == 1
    hb = tm // SUBLANES
    nrow8 = S // SUBLANES
    blk = (lambda c: nblk - 1 - c) if rev else (lambda c: c)
    const = lambda shape: pl.BlockSpec(shape, lambda c: (0,) * len(shape))
    return pl.pallas_call(
        functools.partial(_lru_kernel, rev, tm, nblk),
        grid=(nblk,),
        in_specs=[
            pl.BlockSpec((tm, W), lambda c: (blk(c), 0)),
            pl.BlockSpec((SUBLANES, W), lambda c: (jnp.maximum(blk(c) * hb - 1, 0), 0)),
            pl.BlockSpec((SUBLANES, W), lambda c: (jnp.minimum((blk(c) + 1) * hb, nrow8 - 1), 0)),
            const((CONV_W, W)), const((1, W)),
            const((LRU_BLOCKS, LRU_BLOCK, LRU_BLOCK)), const((1, W)),
            const((LRU_BLOCKS, LRU_BLOCK, LRU_BLOCK)), const((1, W)), const((1, W)),
        ],
        out_specs=pl.BlockSpec((tm, W), lambda c: (blk(c), 0)),
        out_shape=jax.ShapeDtypeStruct((S, W), F32),
        scratch_shapes=[pltpu.VMEM((SUBLANES, W), F32)],
        compiler_params=_cparams(("arbitrary",)),
        name="lru_scan_rev" if rev else "lru_scan_fwd",
    )(h_lru, h_lru, h_lru, p['lru_conv_w'][l], p['lru_conv_b'][l].reshape(1, W),
      p['lru_wa'][l, d].astype(BF16), p['lru_ba'][l, d].reshape(1, W),
      p['lru_wx'][l, d].astype(BF16), p['lru_bx'][l, d].reshape(1, W), p['lru_lambda'][l, d].reshape(1, W))


MLA_QK = 2 * LANES
MLA_Q_SCALE = float((QK_NOPE + QK_ROPE) ** -0.5 * np.log2(np.e))


def _mla_prep_kernel(cq_ref, ckv_ref, kr_ref, pos_ref, invf_ref, qn_ref, kvn_ref, wq_ref, wk_ref, wv_ref,
                     q_o, k_o, v_o):
    def norm(x, g):
        ms = jnp.mean(x * x, axis=-1, keepdims=True)
        return (x * lax.rsqrt(ms + NORM_EPS) * g).astype(BF16)

    q = _dot(norm(cq_ref[...], qn_ref[...]), wq_ref[...]) * MLA_Q_SCALE
    ckv = norm(ckv_ref[...], kvn_ref[...])
    kn = _dot(ckv, wk_ref[...])
    v = _dot(ckv, wv_ref[...])
    ang = pos_ref[...].astype(F32) * invf_ref[...]
    lane = lax.broadcasted_iota(jnp.int32, (1, LANES), 1)
    half = QK_ROPE // 2
    cos = jnp.cos(ang)
    sin = jnp.sin(ang)
    s_lo = jnp.where(lane < half, -sin, 0.0)
    s_hi = jnp.where((lane >= half) & (lane < QK_ROPE), sin, 0.0)

    def rope(t):
        return t * cos + pltpu.roll(t, LANES - half, 1) * s_lo + pltpu.roll(t, half, 1) * s_hi

    kpe = rope(kr_ref[...]).astype(BF16)
    for hh in range(MLA_HEADS):
        q_o[hh, :, 0:LANES] = q[:, hh * MLA_QK:hh * MLA_QK + LANES].astype(BF16)
        q_o[hh, :, LANES:] = rope(q[:, hh * MLA_QK + LANES:(hh + 1) * MLA_QK]).astype(BF16)
        k_o[hh, :, 0:LANES] = kn[:, hh * QK_NOPE:(hh + 1) * QK_NOPE].astype(BF16)
        k_o[hh, :, LANES:] = kpe
        v_o[hh, :, 0:V_HEAD] = v[:, hh * V_HEAD:(hh + 1) * V_HEAD].astype(BF16)
        v_o[hh, :, V_HEAD:] = jnp.ones((v.shape[0], V_HEAD), BF16)


def _mla_prep(h_mla, positions, l, p):
    S = h_mla.shape[0]
    tm = min(512, S)
    half = QK_ROPE // 2
    inv = ROPE_THETA ** (-jnp.arange(0, QK_ROPE, 2, dtype=F32) / QK_ROPE)
    invf = jnp.zeros((1, LANES), F32).at[0, :half].set(inv).at[0, half:QK_ROPE].set(inv)
    wq = p['mla_w_uq'][l].reshape(Q_LORA, MLA_HEADS, QK_NOPE + QK_ROPE)
    wq = jnp.concatenate([wq, jnp.zeros((Q_LORA, MLA_HEADS, MLA_QK - QK_NOPE - QK_ROPE), F32)], axis=-1)
    wq = wq.reshape(Q_LORA, MLA_HEADS * MLA_QK).astype(BF16)
    const = lambda shape: pl.BlockSpec(shape, lambda i: (0,) * len(shape))
    hq = jax.ShapeDtypeStruct((MLA_HEADS, S, MLA_QK), BF16)
    return pl.pallas_call(
        _mla_prep_kernel,
        grid=(S // tm,),
        in_specs=[
            pl.BlockSpec((tm, Q_LORA), lambda i: (i, 0)),
            pl.BlockSpec((tm, KV_LORA), lambda i: (i, Q_LORA // KV_LORA)),
            pl.BlockSpec((tm, LANES), lambda i: (i, MLA_COL_KR // LANES)),
            pl.BlockSpec((tm, 1), lambda i: (i, 0)),
            const((1, LANES)), const((1, Q_LORA)), const((1, KV_LORA)),
            const((Q_LORA, MLA_HEADS * MLA_QK)), const((KV_LORA, MLA_HEADS * QK_NOPE)),
            const((KV_LORA, MLA_HEADS * V_HEAD)),
        ],
        out_specs=[pl.BlockSpec((MLA_HEADS, tm, MLA_QK), lambda i: (0, i, 0)),
                   pl.BlockSpec((MLA_HEADS, tm, MLA_QK), lambda i: (0, i, 0)),
                   pl.BlockSpec((MLA_HEADS, tm, 2 * V_HEAD), lambda i: (0, i, 0))],
        out_shape=[hq, hq, jax.ShapeDtypeStruct((MLA_HEADS, S, 2 * V_HEAD), BF16)],
        compiler_params=_cparams(("parallel",)),
        name="mla_prep",
    )(h_mla, h_mla, h_mla, positions.reshape(S, 1), invf, p['mla_q_norm'][l].reshape(1, Q_LORA),
      p['mla_kv_norm'][l].reshape(1, KV_LORA), wq, p['mla_w_uk'][l].astype(BF16), p['mla_w_uv'][l].astype(BF16))


FLASH_ROWS = 4


def _flash_kernel(nkv, tq, q_ref, k_ref, v_ref, o_ref, m_ref, acc_ref):
    j = pl.program_id(2)

    @pl.when(j == 0)
    def _():
        m_ref[...] = jnp.full_like(m_ref, -jnp.inf)
        acc_ref[...] = jnp.zeros_like(acc_ref)

    tr = tq // FLASH_ROWS
    k = k_ref[...]
    v = v_ref[...]
    scores = [_dot_nt(q_ref[g * tr:(g + 1) * tr, :], k) for g in range(FLASH_ROWS)]
    for g in range(FLASH_ROWS):
        rows = slice(g * tr, (g + 1) * tr)
        s = scores[g]
        m_old = m_ref[rows, :]
        m_new = jnp.maximum(m_old, jnp.max(s, axis=-1, keepdims=True))
        pm = jnp.exp2(s - m_new).astype(BF16)
        acc_ref[rows, :] = jnp.exp2(m_old - m_new) * acc_ref[rows, :] + _dot(pm, v)
        m_ref[rows, :] = m_new

    @pl.when(j == nkv - 1)
    def _():
        o_ref[...] = acc_ref[:, 0:V_HEAD] / acc_ref[:, V_HEAD:]


def _flash_attention(q, k, v):
    S = q.shape[1]
    tq = min(2048, S)
    tk = min(2048, S)
    nkv = S // tk
    return pl.pallas_call(
        functools.partial(_flash_kernel, nkv, tq),
        grid=(MLA_HEADS, S // tq, nkv),
        in_specs=[pl.BlockSpec((None, tq, MLA_QK), lambda h, i, j: (h, i, 0)),
                  pl.BlockSpec((None, tk, MLA_QK), lambda h, i, j: (h, j, 0)),
                  pl.BlockSpec((None, tk, 2 * V_HEAD), lambda h, i, j: (h, j, 0))],
        out_specs=pl.BlockSpec((tq, V_HEAD), lambda h, i, j: (i, h)),
        out_shape=jax.ShapeDtypeStruct((S, MLA_W), F32),
        scratch_shapes=[pltpu.VMEM((tq, 1), F32), pltpu.VMEM((tq, 2 * V_HEAD), F32)],
        compiler_params=_cparams(("parallel", "parallel", "arbitrary")),
        name="mla_flash",
    )(q, k, v)


def _merge_kernel(xn_ref, ya_ref, h0_ref, h1_ref, gb_ref, yc_ref, wg0_ref, wg1_ref, wg2_ref,
                  wa_ref, wb_ref, wc_ref, o_ref):
    xn = xn_ref[...]
    yb = (h0_ref[...] + h1_ref[...]) * jax.nn.gelu(gb_ref[...])
    m = _sigmoid(_dot(xn, wg0_ref[...])) * _dot(ya_ref[...].astype(BF16), wa_ref[...])
    m = m + _sigmoid(_dot(xn, wg1_ref[...])) * _dot(yb.astype(BF16), wb_ref[...])
    m = m + _sigmoid(_dot(xn, wg2_ref[...])) * _dot(yc_ref[...].astype(BF16), wc_ref[...])
    o_ref[...] = m.astype(o_ref.dtype)


def _merge(xn, ya, hs0, hs1, h_lru, yc, w_gates, l, p):
    S = ya.shape[0]
    tm = min(512, S)
    tn = 512
    nj = D_MODEL // tn
    rowa = pl.BlockSpec((tm, RWKV_W), lambda j, i: (i, 0))
    wgate = lambda b: pl.BlockSpec((D_MODEL, tn), lambda j, i: (0, b * nj + j))
    wcol = lambda k: pl.BlockSpec((k, tn), lambda j, i: (0, j))
    return pl.pallas_call(
        _merge_kernel,
        grid=(nj, S // tm),
        in_specs=[pl.BlockSpec((tm, D_MODEL), lambda j, i: (i, 0)),
                  rowa, rowa, rowa, pl.BlockSpec((tm, LRU_W), lambda j, i: (i, 1)),
                  pl.BlockSpec((tm, MLA_W), lambda j, i: (i, 0)),
                  wgate(0), wgate(1), wgate(2), wcol(RWKV_W), wcol(LRU_W), wcol(MLA_W)],
        out_specs=pl.BlockSpec((tm, tn), lambda j, i: (i, j)),
        out_shape=jax.ShapeDtypeStruct((S, D_MODEL), BF16),
        compiler_params=_cparams(("parallel", "parallel")),
        name="merge",
    )(xn, ya, hs0, hs1, h_lru, yc, w_gates, w_gates, w_gates,
      p['wo_rwkv'][l].astype(BF16), p['wo_lru'][l].astype(BF16), p['wo_mla'][l].astype(BF16))


def _router_kernel(x_ref, g_ref, wr_ref, xn_o, lg_o):
    x = x_ref[...]
    ms = jnp.mean(x * x, axis=-1, keepdims=True)
    xn = x * lax.rsqrt(ms + NORM_EPS) * g_ref[...]
    xn_o[...] = xn
    lg_o[...] = _dot3_nt(wr_ref[...], xn)


def _router(x, gain, w_router):
    S = x.shape[0]
    tm = min(512, S)
    return pl.pallas_call(
        _router_kernel,
        grid=(S // tm,),
        in_specs=[pl.BlockSpec((tm, D_MODEL), lambda i: (i, 0)), pl.BlockSpec((1, D_MODEL), lambda i: (0, 0)),
                  pl.BlockSpec((N_EXPERTS, D_MODEL), lambda i: (0, 0))],
        out_specs=[pl.BlockSpec((tm, D_MODEL), lambda i: (i, 0)), pl.BlockSpec((N_EXPERTS, tm), lambda i: (0, i))],
        out_shape=[jax.ShapeDtypeStruct((S, D_MODEL), F32), jax.ShapeDtypeStruct((N_EXPERTS, S), F32)],
        compiler_params=_cparams(("parallel",)),
        name="moe_router",
    )(x, gain.reshape(1, D_MODEL), w_router.T)


def _route_kernel(nb, cap, lg_ref, idx_o, gate_o, aff_ref):
    E = N_EXPERTS
    m = lg_ref[0]
    for e in range(1, E):
        m = jnp.maximum(m, lg_ref[e])
    z = jnp.zeros_like(m)
    for e in range(E):
        z = z + jnp.exp(lg_ref[e] - m)
    for e in range(E):
        aff_ref[e] = jnp.exp(lg_ref[e] - m) / z

    li = lax.broadcasted_iota(jnp.int32, (LANES, LANES), 0)
    lj = lax.broadcasted_iota(jnp.int32, (LANES, LANES), 1)
    upper = jnp.where(li <= lj, 1.0, 0.0).astype(BF16)
    bi = lax.broadcasted_iota(jnp.int32, (nb, nb), 0)
    bj = lax.broadcasted_iota(jnp.int32, (nb, nb), 1)
    lstrict = jnp.where(bj < bi, 1.0, 0.0).astype(BF16)
    qrow = lax.broadcasted_iota(jnp.int32, (1, cap), 1).astype(F32)
    brow = lax.broadcasted_iota(jnp.int32, (nb, cap), 0).astype(F32)
    irow = lax.broadcasted_iota(jnp.int32, (LANES, cap), 0).astype(F32)

    def prefix(mask):
        cum = _dot(mask.astype(BF16), upper)
        tot = cum[:, LANES - 1:LANES]
        off = _dot(lstrict, jnp.broadcast_to(tot, (nb, LANES)).astype(BF16))
        return cum, off, tot

    def expert(e, carry):
        aff = aff_ref[e]
        bits = pltpu.bitcast(aff, jnp.int32)

        def bisect(_, lohi):
            lo, hi = lohi
            mid = lo + lax.shift_right_arithmetic(hi - lo, 1)
            cnt = jnp.sum(jnp.where(bits >= mid, 1.0, 0.0), keepdims=True)
            ok = cnt >= cap
            return jnp.where(ok, mid, lo), jnp.where(ok, hi, mid)

        lo0 = jnp.zeros((1, 1), jnp.int32)
        hi0 = jnp.full((1, 1), 0x7F800000, jnp.int32)
        thr, _ = lax.fori_loop(0, 31, bisect, (lo0, hi0))
        gt = jnp.where(bits > thr, 1.0, 0.0)
        eq = jnp.where(bits == thr, 1.0, 0.0)
        need = cap - jnp.sum(gt, keepdims=True)
        ecum, eoff, _ = prefix(eq)
        sel = gt + eq * jnp.where(ecum + eoff <= need, 1.0, 0.0)
        cum, off, tot = prefix(sel)
        full = jnp.where(off[:, 0:1] + tot <= qrow, 1.0, 0.0)
        nfull = jnp.sum(full, axis=0, keepdims=True)
        offsel = jnp.sum(full * tot, axis=0, keepdims=True)
        onehot_b = jnp.where(brow == nfull, 1.0, 0.0).astype(BF16)
        cum_t = _dot_tn(cum.astype(BF16), onehot_b)
        within = jnp.sum(jnp.where(cum_t <= qrow - offsel, 1.0, 0.0), axis=0, keepdims=True)
        idx_o[e] = (nfull * LANES + within).astype(jnp.int32)
        a1 = aff.astype(BF16)
        r1 = aff - a1.astype(F32)
        a2 = r1.astype(BF16)
        a3 = (r1 - a2.astype(F32)).astype(BF16)
        aff_t = _dot_tn(a1, onehot_b) + _dot_tn(a2, onehot_b) + _dot_tn(a3, onehot_b)
        gate_o[e] = jnp.sum(jnp.where(irow == within, aff_t, 0.0), axis=0, keepdims=True)
        return carry

    lax.fori_loop(0, E, expert, 0)


def _route(logits_t, cap):
    E, S = logits_t.shape
    nb = S // LANES
    return pl.pallas_call(
        functools.partial(_route_kernel, nb, cap),
        out_shape=[jax.ShapeDtypeStruct((E, 1, cap), jnp.int32), jax.ShapeDtypeStruct((E, 1, cap), F32)],
        scratch_shapes=[pltpu.VMEM((E, nb, LANES), F32)],
        compiler_params=pltpu.CompilerParams(vmem_limit_bytes=VMEM_LIMIT),
        name="moe_route",
    )(logits_t.reshape(E, nb, LANES))


DMA_GROUPS = 4
MOE_KC = 256


def _expert_kernel(tq, nsteps, idx_ref, xacc_hbm, xn_hbm, gate_ref, wg_ref, wu_ref, wd_ref, out_hbm,
                   xs_ref, ob_ref, tok_sems, row_sem, sct_sem):
    del xacc_hbm
    n = pl.program_id(0) * pl.num_programs(1) + pl.program_id(1)
    slot = lax.rem(n, 2)
    nxt = jnp.minimum(n + 1, nsteps - 1)
    prv = jnp.maximum(n - 1, 0)

    def hbm_row(ref, tok):
        return ref.at[lax.shift_right_logical(tok, 3), pl.ds(lax.bitwise_and(tok, SUBLANES - 1), 1), :]

    def token_copy(step, buf, i, u):
        tok = idx_ref[step * tq + i * SUBLANES + u]
        return pltpu.make_async_copy(hbm_row(xn_hbm, tok), xs_ref.at[buf, i, pl.ds(u, 1), :], tok_sems.at[buf])

    def row_copy(step, i, u):
        tok = idx_ref[step * tq + i * SUBLANES + u]
        return pltpu.make_async_copy(hbm_row(out_hbm, tok), ob_ref.at[i, pl.ds(u, 1), :], row_sem.at[0])

    def scatter_copy(step, i, u):
        tok = idx_ref[step * tq + i * SUBLANES + u]
        return pltpu.make_async_copy(ob_ref.at[i, pl.ds(u, 1), :], hbm_row(out_hbm, tok), sct_sem.at[0])

    def for_rows(fn):
        def body(i2, c):
            for g in range(DMA_GROUPS):
                for u in range(SUBLANES):
                    fn(i2 * DMA_GROUPS + g, u)
            return c
        lax.fori_loop(0, tq // (SUBLANES * DMA_GROUPS), body, 0)

    def issue_rows(fn, lo, hi):
        for r in range(lo, hi):
            fn(r // SUBLANES, r % SUBLANES)

    @pl.when(n == 0)
    def _():
        for_rows(lambda i, u: token_copy(0, 0, i, u).start())
        for_rows(lambda i, u: row_copy(0, i, u).start())
        for_rows(lambda i, u: row_copy(0, i, u).wait())

    for_rows(lambda i, u: token_copy(n, slot, i, u).wait())
    nk = D_MODEL // MOE_KC
    hg = hu = None
    for k in range(nk):
        lo, hi = k * tq // nk, (k + 1) * tq // nk
        issue_rows(lambda i, u: token_copy(nxt, 1 - slot, i, u).start(), lo, hi)
        issue_rows(lambda i, u: scatter_copy(prv, i, u).start(), lo, hi)
        xk = xs_ref[slot, :, :, k * MOE_KC:(k + 1) * MOE_KC].reshape(tq, MOE_KC).astype(BF16)
        pg = _dot(xk, wg_ref[k * MOE_KC:(k + 1) * MOE_KC, :])
        pu = _dot(xk, wu_ref[k * MOE_KC:(k + 1) * MOE_KC, :])
        hg = pg if hg is None else hg + pg
        hu = pu if hu is None else hu + pu
    hdn = ((hg * _sigmoid(hg)) * hu).astype(BF16)
    for_rows(lambda i, u: scatter_copy(prv, i, u).wait())

    nf = EXPERT_FF // MOE_KC
    ye = None
    for j in range(nf):
        issue_rows(lambda i, u: row_copy(n, i, u).start(), j * tq // nf, (j + 1) * tq // nf)
        py = _dot(hdn[:, j * MOE_KC:(j + 1) * MOE_KC], wd_ref[j * MOE_KC:(j + 1) * MOE_KC, :])
        ye = py if ye is None else ye + py
    for_rows(lambda i, u: row_copy(n, i, u).wait())
    upd = ob_ref[...].reshape(tq, D_MODEL) + ye * gate_ref[...]
    ob_ref[...] = upd.reshape(tq // SUBLANES, SUBLANES, D_MODEL)

    @pl.when(n == nsteps - 1)
    def _():
        for_rows(lambda i, u: scatter_copy(n, i, u).start())
        for_rows(lambda i, u: scatter_copy(n, i, u).wait())
        for_rows(lambda i, u: token_copy(nxt, 1 - slot, i, u).wait())


def _experts(x, xn, idx, gate, l, p):
    S = x.shape[0]
    E, cap = idx.shape
    tq = min(512, cap)
    nt = cap // tq
    grid_spec = pltpu.PrefetchScalarGridSpec(
        num_scalar_prefetch=1,
        grid=(E, nt),
        in_specs=[
            pl.BlockSpec(memory_space=pl.ANY),
            pl.BlockSpec(memory_space=pl.ANY),
            pl.BlockSpec((None, tq, 1), lambda e, t, idx: (e, t, 0)),
            pl.BlockSpec((None, D_MODEL, EXPERT_FF), lambda e, t, idx: (e, 0, 0)),
            pl.BlockSpec((None, D_MODEL, EXPERT_FF), lambda e, t, idx: (e, 0, 0)),
            pl.BlockSpec((None, EXPERT_FF, D_MODEL), lambda e, t, idx: (e, 0, 0)),
        ],
        out_specs=pl.BlockSpec(memory_space=pl.ANY),
        scratch_shapes=[pltpu.VMEM((2, tq // SUBLANES, SUBLANES, D_MODEL), F32),
                        pltpu.VMEM((tq // SUBLANES, SUBLANES, D_MODEL), F32),
                        pltpu.SemaphoreType.DMA((2,)), pltpu.SemaphoreType.DMA((1,)),
                        pltpu.SemaphoreType.DMA((1,))],
    )
    grouped = (S // SUBLANES, SUBLANES, D_MODEL)
    out = pl.pallas_call(
        functools.partial(_expert_kernel, tq, E * nt),
        grid_spec=grid_spec,
        out_shape=jax.ShapeDtypeStruct(grouped, F32),
        input_output_aliases={1: 0},
        compiler_params=_cparams(("arbitrary", "arbitrary")),
        name="moe_experts",
    )(idx.reshape(E * cap), x.reshape(grouped), xn.reshape(grouped), gate.reshape(E, cap, 1),
      p['w_gate'][l].astype(BF16), p['w_up'][l].astype(BF16), p['w_down'][l].astype(BF16))
    return out.reshape(S, D_MODEL)


def _trunk(x, positions, p):
    S = x.shape[0]
    cap = CAPACITY_FACTOR * S // N_EXPERTS
    depth = p['w_in'].shape[0]
    v_first = None
    for l in range(depth):
        xn = _rmsnorm(x, p['norm_mix'][l], BF16)
        h_rwkv, h_lru, h_mla, w_gates = _in_projection(xn, p['w_in'][l], p['rwkv_vres_down'][l - 1] if l > 0 else None)
        r, v, kk, lw, kd, a, g, bonus = _rwkv_prep(h_rwkv, h_mla, l, p, v_first)
        if l == 0:
            v_first = v
        yf, yb = _rwkv_scan(r, v, kk, lw, kd, a)
        ya = _rwkv_post(yf, yb, bonus, g, p['rwkv_lnx_w'][l], p['rwkv_lnx_b'][l])
        hs0 = _lru_scan(h_lru, l, p, 0)
        hs1 = _lru_scan(h_lru, l, p, 1)
        q, k, vv = _mla_prep(h_mla, positions, l, p)
        yc = _flash_attention(q, k, vv)
        merged = _merge(xn, ya, hs0, hs1, h_lru, yc, w_gates, l, p)
        x = _matmul(merged, p['w_out'][l].astype(BF16), D_MODEL, res=x, name="outproj")
        xn2, logits_t = _router(x, p['norm_ffn'][l], p['w_router'][l])
        idx, gate = _route(logits_t, cap)
        x = _experts(x, xn2, idx.reshape(N_EXPERTS, cap), gate.reshape(N_EXPERTS, cap), l, p)
    return _rmsnorm(x, p['norm_final'], F32)


def kernel(x, positions, norm_mix, w_in, rwkv_shift_mu, rwkv_w0, rwkv_w2, rwkv_a0, rwkv_a2, rwkv_g2, rwkv_k_k,
           rwkv_k_a, rwkv_r_k, rwkv_lnx_w, rwkv_lnx_b, rwkv_vres_down, rwkv_vres_up, rwkv_vres_b, lru_conv_w,
           lru_conv_b, lru_wa, lru_ba, lru_wx, lru_bx, lru_lambda, mla_q_norm, mla_kv_norm, mla_w_uq, mla_w_uk,
           mla_w_uv, wo_rwkv, wo_lru, wo_mla, w_out, norm_ffn, w_router, w_gate, w_up, w_down, norm_final):
    p = dict(norm_mix=norm_mix, w_in=w_in, rwkv_shift_mu=rwkv_shift_mu, rwkv_w0=rwkv_w0, rwkv_w2=rwkv_w2,
             rwkv_a0=rwkv_a0, rwkv_a2=rwkv_a2, rwkv_g2=rwkv_g2, rwkv_k_k=rwkv_k_k, rwkv_k_a=rwkv_k_a,
             rwkv_r_k=rwkv_r_k, rwkv_lnx_w=rwkv_lnx_w, rwkv_lnx_b=rwkv_lnx_b, rwkv_vres_down=rwkv_vres_down,
             rwkv_vres_up=rwkv_vres_up, rwkv_vres_b=rwkv_vres_b, lru_conv_w=lru_conv_w, lru_conv_b=lru_conv_b,
             lru_wa=lru_wa, lru_ba=lru_ba, lru_wx=lru_wx, lru_bx=lru_bx, lru_lambda=lru_lambda,
             mla_q_norm=mla_q_norm, mla_kv_norm=mla_kv_norm, mla_w_uq=mla_w_uq, mla_w_uk=mla_w_uk,
             mla_w_uv=mla_w_uv, wo_rwkv=wo_rwkv, wo_lru=wo_lru, wo_mla=wo_mla, w_out=w_out, norm_ffn=norm_ffn,
             w_router=w_router, w_gate=w_gate, w_up=w_up, w_down=w_down, norm_final=norm_final)
    B, S, D = x.shape
    outs = [_trunk(x[b], positions[b], p) for b in range(B)]
    return jnp.stack(outs, axis=0)
```
